```python
import math
import jax, jax.numpy as jnp
from jax import lax
import numpy as np

D_MODEL = 1024
BATCH = 4
SEQ = 8192
DEPTH = 1

N_META = 16
CHUNK = 128
META_PAD = (-N_META) % CHUNK
RET_HEADS = 8
RET_DK = 64
RET_DV = 128
RET_QK_W = RET_HEADS * RET_DK
RET_V_W = RET_HEADS * RET_DV
MLA_HEADS = 8
MLA_Q_RANK = 384
MLA_KV_RANK = 256
MLA_NOPE = 64
MLA_ROPE = 32
MLA_DV = 64
MLA_QK = MLA_NOPE + MLA_ROPE
D_FF = ((-(-8 * D_MODEL // 3) + 255) // 256) * 256
ROPE_BASE = 10000.0
RMS_EPS = 1e-6
GN_EPS = 1e-5

SPLIT_SIZES = (RET_QK_W, RET_QK_W, RET_V_W, RET_V_W, MLA_Q_RANK, MLA_KV_RANK, MLA_ROPE, D_MODEL, D_MODEL)
SPLIT_IDX = tuple(int(s) for s in np.cumsum(SPLIT_SIZES)[:-1])
D_IN = int(sum(SPLIT_SIZES))

kernel_name = "hybrid_retention_mla_gated_encoder"


def _rmsnorm(x, w):
    xf = x.astype(jnp.float32)
    y = xf * lax.rsqrt(jnp.mean(xf * xf, axis=-1, keepdims=True) + RMS_EPS)
    return (y * w.astype(jnp.float32)).astype(x.dtype)


def _rope_tables(pos, dim):
    half = dim // 2
    inv = ROPE_BASE ** (-jnp.arange(half, dtype=jnp.float32) / half)
    ang = pos.astype(jnp.float32)[:, None] * inv[None, :]
    return jnp.cos(ang), jnp.sin(ang)


def _apply_rope(x, cos, sin):
    half = x.shape[-1] // 2
    xf = x.astype(jnp.float32)
    x1, x2 = xf[..., :half], xf[..., half:]
    return jnp.concatenate([x1 * cos - x2 * sin, x1 * sin + x2 * cos], axis=-1).astype(x.dtype)


def _retention_dir(q, k, v, log_gamma, inclusive):
    B, Lp, H, dk = q.shape
    dv = v.shape[-1]
    n = Lp // CHUNK
    qc = q.reshape(B, n, CHUNK, H, dk)
    kc = k.reshape(B, n, CHUNK, H, dk)
    vc = v.reshape(B, n, CHUNK, H, dv)
    lg = log_gamma.astype(jnp.float32)
    idx = jnp.arange(CHUNK, dtype=jnp.float32)
    rel = idx[:, None] - idx[None, :]
    mask = rel >= 0 if inclusive else rel > 0
    dmask = jnp.exp(jnp.where(mask[None], lg[:, None, None] * rel[None], -jnp.inf)).astype(q.dtype)
    scores = jnp.einsum('bnihd,bnjhd->bnhij', qc, kc) * dmask
    inner = jnp.einsum('bnhij,bnjhe->bnihe', scores, vc)
    w_k = jnp.exp(lg[None, :] * (CHUNK - 1 - idx)[:, None]).astype(q.dtype)
    incr = jnp.einsum('bnjhd,jh,bnjhe->nbhde', kc, w_k, vc)
    g_chunk = jnp.exp(lg * CHUNK).astype(incr.dtype)[None, :, None, None]

    def step(state, u):
        return g_chunk * state + u, state

    state0 = jnp.zeros((B, H, dk, dv), incr.dtype)
    _, states_prev = lax.scan(step, state0, incr)
    w_q = jnp.exp(lg[None, :] * (idx + 1.0)[:, None]).astype(q.dtype)
    cross = jnp.einsum('bnihd,ih,nbhde->bnihe', qc, w_q, states_prev)
    return (inner + cross).reshape(B, Lp, H, dv)


def _bidir_retention(q, k, v, decay_f, decay_b):
    lg_f = -jnp.exp(decay_f.astype(jnp.float32))
    lg_b = -jnp.exp(decay_b.astype(jnp.float32))
    pad = ((0, 0), (META_PAD, 0), (0, 0), (0, 0))
    qp, kp, vp = jnp.pad(q, pad), jnp.pad(k, pad), jnp.pad(v, pad)
    fwd = _retention_dir(qp, kp, vp, lg_f, True)
    bwd = jnp.flip(_retention_dir(jnp.flip(qp, 1), jnp.flip(kp, 1), jnp.flip(vp, 1), lg_b, False), 1)
    return (fwd + bwd)[:, META_PAD:]


def _head_group_norm(y, w):
    B, L, H, dv = y.shape
    yf = y.astype(jnp.float32)
    mu = jnp.mean(yf, axis=-1, keepdims=True)
    var = jnp.mean(jnp.square(yf - mu), axis=-1, keepdims=True)
    yn = ((yf - mu) * lax.rsqrt(var + GN_EPS)).reshape(B, L, H * dv)
    return (yn * w.astype(jnp.float32)).astype(y.dtype)


def _mla_attention(q_nope, q_rope, k_nope, k_rope, v):
    B, L, H, _ = q_nope.shape
    Lp = L + META_PAD
    nb = Lp // CHUNK
    scale = MLA_QK ** -0.5

    def to_blocks(t):
        t = jnp.pad(t, ((0, 0), (META_PAD, 0), (0, 0), (0, 0)))
        return t.reshape(B, nb, CHUNK, H, t.shape[-1]).swapaxes(0, 1)

    def attend(qb):
        qn, qr = qb
        s = jnp.einsum('bqhd,bkhd->bhqk', qn, k_nope) + jnp.einsum('bqhr,bkr->bhqk', qr, k_rope)
        p = jax.nn.softmax(s.astype(jnp.float32) * scale, axis=-1).astype(v.dtype)
        return jnp.einsum('bhqk,bkhd->bqhd', p, v)

    out = lax.map(attend, (to_blocks(q_nope), to_blocks(q_rope)))
    return out.swapaxes(0, 1).reshape(B, Lp, H, MLA_DV)[:, META_PAD:]


def _mixer(u, w_in, decay_f, decay_b, gn_w, w_ret_out, q_norm_w, w_uq, kv_norm_w, w_uk, w_uv,
           w_mla_out, w_o, ret_cos, ret_sin, mla_cos, mla_sin):
    B, L, _ = u.shape
    proj = u @ w_in
    rq, rk, rv, rg, a_cq, a_ckv, a_kr, a_gret, a_gmla = jnp.split(proj, SPLIT_IDX, axis=-1)

    rq = _apply_rope(rq.reshape(B, L, RET_HEADS, RET_DK), ret_cos[:, None, :], ret_sin[:, None, :])
    rk = _apply_rope(rk.reshape(B, L, RET_HEADS, RET_DK), ret_cos[:, None, :], ret_sin[:, None, :]) * (RET_DK ** -0.5)
    rv = rv.reshape(B, L, RET_HEADS, RET_DV)
    y_ret = _head_group_norm(_bidir_retention(rq, rk, rv, decay_f, decay_b), gn_w)
    y_ret = (jax.nn.silu(rg) * y_ret) @ w_ret_out

    c_q = _rmsnorm(a_cq, q_norm_w)
    q = (c_q @ w_uq).reshape(B, L, MLA_HEADS, MLA_QK)
    q_nope = q[..., :MLA_NOPE]
    q_rope = _apply_rope(q[..., MLA_NOPE:], mla_cos[:, None, :], mla_sin[:, None, :])
    c_kv = _rmsnorm(a_ckv, kv_norm_w)
    k_nope = (c_kv @ w_uk).reshape(B, L, MLA_HEADS, MLA_NOPE)
    v = (c_kv @ w_uv).reshape(B, L, MLA_HEADS, MLA_DV)
    k_rope = _apply_rope(a_kr, mla_cos, mla_sin)
    y_mla = _mla_attention(q_nope, q_rope, k_nope, k_rope, v).reshape(B, L, MLA_HEADS * MLA_DV) @ w_mla_out

    merged = jax.nn.sigmoid(a_gret) * y_ret + jax.nn.sigmoid(a_gmla) * y_mla
    return merged @ w_o


def _swiglu(u, w_gate, w_up, w_down):
    return (jax.nn.silu(u @ w_gate) * (u @ w_up)) @ w_down


def setup_inputs(seed: int = 0) -> dict:
    key = jax.random.key(seed)
    ks = jax.random.split(key, 24)
    f32 = jnp.float32

    def nrm(k, shape, fan_in):
        return jax.random.normal(k, shape, f32) * (fan_in ** -0.5)

    def gain(k, shape):
        return 1.0 + 0.02 * jax.random.normal(k, shape, f32)

    h_idx = jnp.arange(RET_HEADS, dtype=f32)
    gamma = 1.0 - 2.0 ** (-5.0 - h_idx)
    decay_base = jnp.log(-jnp.log(gamma))
    return {
        'x': jax.random.normal(ks[0], (BATCH, SEQ, D_MODEL), f32),
        'meta_tokens': jax.random.normal(ks[1], (N_META, D_MODEL), f32),
        'norm_mix_w': gain(ks[2], (DEPTH, D_MODEL)),
        'w_in': nrm(ks[3], (DEPTH, D_MODEL, D_IN), D_MODEL),
        'ret_decay_fwd': decay_base[None] + 0.05 * jax.random.normal(ks[4], (DEPTH, RET_HEADS), f32),
        'ret_decay_bwd': decay_base[None] + 0.05 * jax.random.normal(ks[5], (DEPTH, RET_HEADS), f32),
        'ret_gn_w': gain(ks[6], (DEPTH, RET_V_W)),
        'w_ret_out': nrm(ks[7], (DEPTH, RET_V_W, D_MODEL), RET_V_W),
        'mla_q_norm_w': gain(ks[8], (DEPTH, MLA_Q_RANK)),
        'w_uq': nrm(ks[9], (DEPTH, MLA_Q_RANK, MLA_HEADS * MLA_QK), MLA_Q_RANK),
        'mla_kv_norm_w': gain(ks[10], (DEPTH, MLA_KV_RANK)),
        'w_uk': nrm(ks[11], (DEPTH, MLA_KV_RANK, MLA_HEADS * MLA_NOPE), MLA_KV_RANK),
        'w_uv': nrm(ks[12], (DEPTH, MLA_KV_RANK, MLA_HEADS * MLA_DV), MLA_KV_RANK),
        'w_mla_out': nrm(ks[13], (DEPTH, MLA_HEADS * MLA_DV, D_MODEL), MLA_HEADS * MLA_DV),
        'w_o': nrm(ks[14], (DEPTH, D_MODEL, D_MODEL), D_MODEL),
        'norm_ffn_w': gain(ks[15], (DEPTH, D_MODEL)),
        'w_ffn_gate': nrm(ks[16], (DEPTH, D_MODEL, D_FF), D_MODEL),
        'w_ffn_up': nrm(ks[17], (DEPTH, D_MODEL, D_FF), D_MODEL),
        'w_ffn_down': nrm(ks[18], (DEPTH, D_FF, D_MODEL), D_FF),
        'norm_final_w': gain(ks[19], (D_MODEL,)),
    }


def reference(x, meta_tokens, norm_mix_w, w_in, ret_decay_fwd, ret_decay_bwd, ret_gn_w, w_ret_out,
              mla_q_norm_w, w_uq, mla_kv_norm_w, w_uk, w_uv, w_mla_out, w_o, norm_ffn_w,
              w_ffn_gate, w_ffn_up, w_ffn_down, norm_final_w):
    B, S, D = x.shape
    L = S + N_META
    meta = jnp.broadcast_to(meta_tokens.astype(x.dtype)[None], (B, N_META, D))
    h = jnp.concatenate([meta, x], axis=1)
    pos = jnp.arange(L)
    ret_cos, ret_sin = _rope_tables(pos, RET_DK)
    mla_cos, mla_sin = _rope_tables(pos, MLA_ROPE)
    for l in range(DEPTH):
        h = h + _mixer(_rmsnorm(h, norm_mix_w[l]), w_in[l], ret_decay_fwd[l], ret_decay_bwd[l], ret_gn_w[l],
                       w_ret_out[l], mla_q_norm_w[l], w_uq[l], mla_kv_norm_w[l], w_uk[l], w_uv[l],
                       w_mla_out[l], w_o[l], ret_cos, ret_sin, mla_cos, mla_sin)
        h = h + _swiglu(_rmsnorm(h, norm_ffn_w[l]), w_ffn_gate[l], w_ffn_up[l], w_ffn_down[l])
    h = _rmsnorm(h, norm_final_w)
    return h[:, N_META:]
```

```python
import functools
import math

import jax
import jax.numpy as jnp
from jax import lax
from jax.experimental import pallas as pl
from jax.experimental.pallas import tpu as pltpu

N_META = 16
CHUNK = 128
RET_HEADS = 8
RET_DK = 64
RET_DV = 128
MLA_HEADS = 8
MLA_NOPE = 64
MLA_ROPE = 32
MLA_DV = 64
MLA_QK = MLA_NOPE + MLA_ROPE
ROPE_BASE = 10000.0
RMS_EPS = 1e-6
GN_EPS = 1e-5

LANES = 128
ONES_LANE = MLA_DV
VMEM_LIMIT = 56 * 1024 * 1024

F32 = jnp.float32
BF16 = jnp.bfloat16


def _rms(x, w):
    return x * lax.rsqrt(jnp.mean(x * x, axis=-1, keepdims=True) + RMS_EPS) * w


def _sigmoid(x):
    return 1.0 / (1.0 + jnp.exp(-x))


def _rope_group(x, cos, sin_lo, sin_hi, half):
    return (x * cos + pltpu.roll(x, LANES - half, axis=1) * sin_lo
            + pltpu.roll(x, half, axis=1) * sin_hi)


def _in_proj_kernel(x_ref, nw_ref, w_rq, w_rk, w_rv, w_rg, w_cq, w_ckv, w_kr, w_gr, w_gm,
                    qnw_ref, w_uq, kvnw_ref, w_uk, w_uv,
                    rc_ref, rlo_ref, rhi_ref, mc_ref, mlo_ref, mhi_ref,
                    rq_o, rk_o, rv_o, sg_o, q_o, k_o, v_o, gr_o, gm_o, *, q_scale):
    u = _rms(x_ref[0], nw_ref[...]).astype(BF16)

    def mm(a, w):
        return jnp.dot(a, w[...], preferred_element_type=F32)

    rc, rlo, rhi = rc_ref[...], rlo_ref[...], rhi_ref[...]
    mc, mlo, mhi = mc_ref[...], mlo_ref[...], mhi_ref[...]

    rq = mm(u, w_rq)
    rk = mm(u, w_rk)
    for c in range(rq.shape[1] // LANES):
        sl = slice(c * LANES, (c + 1) * LANES)
        rq_o[0, :, sl] = _rope_group(rq[:, sl], rc, rlo, rhi, RET_DK // 2).astype(BF16)
        rk_o[0, :, sl] = (_rope_group(rk[:, sl], rc, rlo, rhi, RET_DK // 2)
                          * (RET_DK ** -0.5)).astype(BF16)
    rv_o[0] = mm(u, w_rv).astype(BF16)
    rg = mm(u, w_rg)
    sg_o[0] = (rg * _sigmoid(rg)).astype(BF16)
    gr_o[0] = _sigmoid(mm(u, w_gr)).astype(BF16)
    gm_o[0] = _sigmoid(mm(u, w_gm)).astype(BF16)

    cq = _rms(mm(u, w_cq), qnw_ref[...]).astype(BF16)
    q = mm(cq, w_uq)
    ckv = _rms(mm(u, w_ckv), kvnw_ref[...]).astype(BF16)
    kn = mm(ckv, w_uk)
    kr = _rope_group(mm(u, w_kr), mc, mlo, mhi, MLA_ROPE // 2)
    for h in range(MLA_HEADS):
        sl = slice(h * LANES, (h + 1) * LANES)
        q_o[0, :, sl] = (_rope_group(q[:, sl], mc, mlo, mhi, MLA_ROPE // 2) * q_scale).astype(BF16)
        k_o[0, :, sl] = (kn[:, sl] + kr).astype(BF16)
    v = mm(ckv, w_uv)
    lane = lax.broadcasted_iota(jnp.int32, v.shape, 1)
    v_o[0] = jnp.where(lane % LANES == ONES_LANE, 1.0, v).astype(BF16)


def _in_proj(rows, tabs, nw, wts, q_scale, tm):
    nb, r, d = rows.shape
    grid = (r // tm, nb)
    const = lambda i, b: (0, 0)

    def wspec(w):
        return pl.BlockSpec(w.shape, const, pipeline_mode=pl.Buffered(1))

    row_spec = lambda width: pl.BlockSpec((1, tm, width), lambda i, b: (b, i, 0))
    tab_spec = pl.BlockSpec((tm, LANES), lambda i, b: (i, 0))
    (w_rq, w_rk, w_rv, w_rg, w_cq, w_ckv, w_kr, w_gr, w_gm, qnw, w_uq, kvnw, w_uk, w_uv) = wts
    out_widths = (w_rq.shape[1], w_rk.shape[1], w_rv.shape[1], w_rg.shape[1],
                  w_uq.shape[1], w_uk.shape[1], w_uv.shape[1], w_gr.shape[1], w_gm.shape[1])
    return pl.pallas_call(
        functools.partial(_in_proj_kernel, q_scale=q_scale),
        grid=grid,
        in_specs=[row_spec(d), wspec(nw)] + [wspec(w) for w in wts] + [tab_spec] * 6,
        out_specs=[row_spec(w) for w in out_widths],
        out_shape=[jax.ShapeDtypeStruct((nb, r, w), BF16) for w in out_widths],
        compiler_params=pltpu.CompilerParams(
            dimension_semantics=("arbitrary", "arbitrary"), vmem_limit_bytes=VMEM_LIMIT),
        name="in_proj",
    )(rows, nw, *wts, *tabs)


def _retention_kernel(lgf_ref, lgb_ref, q_ref, k_ref, v_ref, sg_ref, km_ref, vm_ref, gnw_ref,
                      o_ref, sb_all, sf_run, sb_run, dmask, wqf, wqb, wkf, wkb, gcf, gcb,
                      *, cpb):
    p = pl.program_id(1)
    n = pl.program_id(2)
    nblk = pl.num_programs(2)
    c_len = CHUNK
    row = lax.broadcasted_iota(jnp.int32, (c_len, LANES), 0).astype(F32)
    col = lax.broadcasted_iota(jnp.int32, (c_len, LANES), 1).astype(F32)
    lane_i = lax.broadcasted_iota(jnp.int32, (c_len, LANES), 1)
    tn = (((0,), (0,)), ((), ()))

    def head_lanes(h):
        lo = (h % 2) * RET_DK
        return (lane_i >= lo) & (lane_i < lo + RET_DK)

    @pl.when((p == 0) & (n == 0))
    def _tables():
        rel = row - col
        for h in range(RET_HEADS):
            lf = lgf_ref[h]
            lb = lgb_ref[h]
            dmask[h] = jnp.exp(jnp.where(rel >= 0, lf * rel, -lb * rel))
            hm = head_lanes(h)
            wqf[h] = jnp.where(hm, jnp.exp(lf * (row + 1.0)), 0.0)
            wqb[h] = jnp.where(hm, jnp.exp(lb * (c_len - row)), 0.0)
            wkf[h] = jnp.exp(lf * (c_len - 1.0 - row))
            wkb[h] = jnp.exp(lb * row)
            gcf[h] = jnp.exp(jnp.full((8, LANES), lf * c_len, F32))
            gcb[h] = jnp.exp(jnp.full((8, LANES), lb * c_len, F32))
            sb_run[h] = jnp.zeros((c_len, RET_DV), F32)

    def k_pair(c, h):
        return k_ref[0, c * c_len:(c + 1) * c_len, (h // 2) * LANES:(h // 2 + 1) * LANES]

    def v_head(c, h):
        return v_ref[0, c * c_len:(c + 1) * c_len, h * RET_DV:(h + 1) * RET_DV]

    @pl.when(p == 0)
    def _backward_states():
        blk = nblk - 1 - n
        for c in range(cpb - 1, -1, -1):
            g = blk * cpb + c
            for h in range(RET_HEADS):
                sb_all[g, h] = sb_run[h].astype(BF16)
                kw = (k_pair(c, h).astype(F32) * wkb[h]).astype(BF16)
                kv = lax.dot_general(kw, v_head(c, h), tn, preferred_element_type=F32)
                sb_run[h] = gcb[h][0:1, :] * sb_run[h] + kv

    @pl.when(p == 1)
    def _forward_outputs():
        @pl.when(n == 0)
        def _meta_state():
            mrow = lax.broadcasted_iota(jnp.int32, (N_META, LANES), 0).astype(F32)
            for h in range(RET_HEADS):
                wm = jnp.exp(lgf_ref[h] * (N_META - 1.0 - mrow))
                kmw = (km_ref[:, (h // 2) * LANES:(h // 2 + 1) * LANES].astype(F32) * wm).astype(BF16)
                vmh = vm_ref[:, h * RET_DV:(h + 1) * RET_DV]
                sf_run[h] = lax.dot_general(kmw, vmh, tn, preferred_element_type=F32)

        for c in range(cpb):
            g = n * cpb + c
            rows = slice(c * c_len, (c + 1) * c_len)
            for h in range(RET_HEADS):
                qp = q_ref[0, rows, (h // 2) * LANES:(h // 2 + 1) * LANES].astype(F32)
                kp = k_pair(c, h)
                vh = v_head(c, h)
                qh = jnp.where(head_lanes(h), qp, 0.0).astype(BF16)
                s = lax.dot_general(qh, kp, (((1,), (1,)), ((), ())), preferred_element_type=F32)
                o = jnp.dot((s * dmask[h]).astype(BF16), vh, preferred_element_type=F32)
                o = o + jnp.dot((qp * wqf[h]).astype(BF16), sf_run[h].astype(BF16),
                                preferred_element_type=F32)
                o = o + jnp.dot((qp * wqb[h]).astype(BF16), sb_all[g, h],
                                preferred_element_type=F32)
                kw = (kp.astype(F32) * wkf[h]).astype(BF16)
                kv = lax.dot_general(kw, vh, tn, preferred_element_type=F32)
                sf_run[h] = gcf[h][0:1, :] * sf_run[h] + kv
                mu = jnp.mean(o, axis=-1, keepdims=True)
                d = o - mu
                var = jnp.mean(d * d, axis=-1, keepdims=True)
                yn = d * lax.rsqrt(var + GN_EPS)
                hs = slice(h * RET_DV, (h + 1) * RET_DV)
                y = yn * gnw_ref[:, hs] * sg_ref[0, rows, hs].astype(F32)
                o_ref[0, rows, hs] = y.astype(BF16)


def _retention(lgf, lgb, rq, rk, rv, sg, km, vm, gnw, cpb):
    b, s, qk_w = rq.shape
    v_w = rv.shape[2]
    rb = cpb * CHUNK
    nblk = s // rb
    n_chunks = s // CHUNK
    q_idx = lambda bi, p, n, *_: (bi, jnp.where(p == 0, 0, n), 0)
    kv_idx = lambda bi, p, n, *_: (bi, jnp.where(p == 0, nblk - 1 - n, n), 0)
    const = lambda bi, p, n, *_: (0, 0)
    tab = lambda rows_: pltpu.VMEM((RET_HEADS, rows_, LANES), F32)
    grid_spec = pltpu.PrefetchScalarGridSpec(
        num_scalar_prefetch=2,
        grid=(b, 2, nblk),
        in_specs=[
            pl.BlockSpec((1, rb, qk_w), q_idx),
            pl.BlockSpec((1, rb, qk_w), kv_idx),
            pl.BlockSpec((1, rb, v_w), kv_idx),
            pl.BlockSpec((1, rb, v_w), q_idx),
            pl.BlockSpec(km.shape, const),
            pl.BlockSpec(vm.shape, const),
            pl.BlockSpec(gnw.shape, const),
        ],
        out_specs=pl.BlockSpec((1, rb, v_w), q_idx),
        scratch_shapes=[
            pltpu.VMEM((n_chunks, RET_HEADS, LANES, RET_DV), BF16),
            tab(LANES), tab(LANES),
            tab(CHUNK), tab(CHUNK), tab(CHUNK), tab(CHUNK), tab(CHUNK),
            tab(8), tab(8),
        ],
    )
    return pl.pallas_call(
        functools.partial(_retention_kernel, cpb=cpb),
        grid_spec=grid_spec,
        out_shape=jax.ShapeDtypeStruct((b, s, v_w), BF16),
        compiler_params=pltpu.CompilerParams(
            dimension_semantics=("arbitrary", "arbitrary", "arbitrary"),
            vmem_limit_bytes=VMEM_LIMIT),
        name="retention",
    )(lgf, lgb, rq, rk, rv, sg, km, vm, gnw)


def _attn_kernel(q_ref, k_ref, v_ref, km_ref, vm_ref, o_ref, *, tk):
    q = q_ref[0]
    nt = (((1,), (1,)), ((), ()))
    s0 = lax.dot_general(q, km_ref[...], nt, preferred_element_type=F32)
    m0 = jnp.max(s0, axis=1, keepdims=True)
    acc0 = jnp.dot(jnp.exp2(s0 - m0).astype(BF16), vm_ref[...], preferred_element_type=F32)

    def body(j, carry):
        m, acc = carry
        start = pl.multiple_of(j * tk, tk)
        kt = k_ref[0, pl.ds(start, tk), :]
        vt = v_ref[0, pl.ds(start, tk), :]
        s = lax.dot_general(q, kt, nt, preferred_element_type=F32)
        m_new = jnp.maximum(m, jnp.max(s, axis=1, keepdims=True))
        p = jnp.exp2(s - m_new).astype(BF16)
        acc = jnp.exp2(m - m_new) * acc + jnp.dot(p, vt, preferred_element_type=F32)
        return m_new, acc

    _, acc = lax.fori_loop(0, k_ref.shape[1] // tk, body, (m0, acc0))
    o_ref[0] = (acc / acc[:, ONES_LANE:ONES_LANE + 1]).astype(BF16)


def _attention(q, k, v, km, vm, tq, tk):
    b, s, w = q.shape
    heads = w // LANES
    return pl.pallas_call(
        functools.partial(_attn_kernel, tk=tk),
        grid=(b, heads, s // tq),
        in_specs=[
            pl.BlockSpec((1, tq, LANES), lambda bi, h, i: (bi, i, h)),
            pl.BlockSpec((1, s, LANES), lambda bi, h, i: (bi, 0, h)),
            pl.BlockSpec((1, s, LANES), lambda bi, h, i: (bi, 0, h)),
            pl.BlockSpec((N_META, LANES), lambda bi, h, i: (0, h)),
            pl.BlockSpec((N_META, LANES), lambda bi, h, i: (0, h)),
        ],
        out_specs=pl.BlockSpec((1, tq, LANES), lambda bi, h, i: (bi, i, h)),
        out_shape=jax.ShapeDtypeStruct((b, s, w), BF16),
        compiler_params=pltpu.CompilerParams(
            dimension_semantics=("arbitrary", "arbitrary", "arbitrary"),
            vmem_limit_bytes=VMEM_LIMIT),
        name="mla_attention",
    )(q, k, v, km, vm)


def _merge_ffn_kernel(x_ref, yr_ref, ya_ref, gr_ref, gm_ref, w_ro, w_mo, w_o, nfw_ref,
                      w_g, w_u, w_d, nfin_ref, o_ref):
    def mm(a, w):
        return jnp.dot(a, w[...], preferred_element_type=F32)

    merged = (gr_ref[0].astype(F32) * mm(yr_ref[0], w_ro)
              + gm_ref[0].astype(F32) * mm(ya_ref[0], w_mo))
    h1 = x_ref[0] + mm(merged.astype(BF16), w_o)
    u = _rms(h1, nfw_ref[...]).astype(BF16)
    g = mm(u, w_g)
    act = (g * _sigmoid(g) * mm(u, w_u)).astype(BF16)
    h2 = h1 + mm(act, w_d)
    o_ref[0] = _rms(h2, nfin_ref[...])


def _merge_ffn(x, yr, ya, gr, gm, wts, tm):
    b, s, d = x.shape
    const = lambda bi, i: (0, 0)
    wspec = lambda w: pl.BlockSpec(w.shape, const, pipeline_mode=pl.Buffered(1))
    row = pl.BlockSpec((1, tm, d), lambda bi, i: (bi, i, 0))
    return pl.pallas_call(
        _merge_ffn_kernel,
        grid=(b, s // tm),
        in_specs=[row] * 5 + [wspec(w) for w in wts],
        out_specs=row,
        out_shape=jax.ShapeDtypeStruct((b, s, d), F32),
        compiler_params=pltpu.CompilerParams(
            dimension_semantics=("arbitrary", "arbitrary"), vmem_limit_bytes=VMEM_LIMIT),
        name="merge_ffn",
    )(x, yr, ya, gr, gm, *wts)


def _rope_tables(n_pos):
    pos = jnp.arange(n_pos, dtype=F32)[:, None]
    lane = jnp.arange(LANES)

    def angles(half):
        inv = ROPE_BASE ** (-jnp.arange(half, dtype=F32) / half)
        return pos * inv[None, :]

    half = RET_DK // 2
    ang = angles(half)[:, lane % half]
    first = (lane % RET_DK) < half
    r_cos = jnp.cos(ang)
    r_lo = jnp.where(first, -jnp.sin(ang), 0.0)
    r_hi = jnp.where(first, 0.0, jnp.sin(ang))
    half = MLA_ROPE // 2
    ang = angles(half)[:, lane % half]
    in_rope = (lane >= MLA_NOPE) & (lane < MLA_QK)
    first = in_rope & (lane < MLA_NOPE + half)
    second = in_rope & (lane >= MLA_NOPE + half)
    m_cos = jnp.where(in_rope, jnp.cos(ang), 1.0)
    m_lo = jnp.where(first, -jnp.sin(ang), 0.0)
    m_hi = jnp.where(second, jnp.sin(ang), 0.0)
    return r_cos, r_lo, r_hi, m_cos, m_lo, m_hi


def _pad_heads(w, heads, width):
    k = w.shape[0]
    w = w.reshape(k, heads, width)
    return jnp.pad(w, ((0, 0), (0, 0), (0, LANES - width))).reshape(k, heads * LANES)


def kernel(x, meta_tokens, norm_mix_w, w_in, ret_decay_fwd, ret_decay_bwd, ret_gn_w, w_ret_out, mla_q_norm_w, w_uq, mla_kv_norm_w, w_uk, w_uv, w_mla_out, w_o, norm_ffn_w, w_ffn_gate, w_ffn_up, w_ffn_down, norm_final_w):
    b, s, d = x.shape
    assert w_in.shape[0] == 1, "one layer: the meta rows are dropped after it"
    ret_qk_w = RET_HEADS * RET_DK
    ret_v_w = RET_HEADS * RET_DV
    q_rank = w_uq.shape[1]
    kv_rank = w_uk.shape[1]
    sizes = (ret_qk_w, ret_qk_w, ret_v_w, ret_v_w, q_rank, kv_rank, MLA_ROPE, d, d)
    assert sum(sizes) == w_in.shape[2]
    bounds = [0]
    for sz in sizes:
        bounds.append(bounds[-1] + sz)
    w_rq, w_rk, w_rv, w_rg, w_cq, w_ckv, w_kr, w_gr, w_gm = (
        w_in[0][:, lo:hi].astype(BF16) for lo, hi in zip(bounds[:-1], bounds[1:]))
    w_kr = jnp.pad(w_kr, ((0, 0), (MLA_NOPE, LANES - MLA_QK)))
    row2 = lambda a: a.reshape(1, -1).astype(F32)
    wts1 = (w_rq, w_rk, w_rv, w_rg, w_cq, w_ckv, w_kr, w_gr, w_gm,
            row2(mla_q_norm_w[0]), _pad_heads(w_uq[0], MLA_HEADS, MLA_QK).astype(BF16),
            row2(mla_kv_norm_w[0]), _pad_heads(w_uk[0], MLA_HEADS, MLA_NOPE).astype(BF16),
            _pad_heads(w_uv[0], MLA_HEADS, MLA_DV).astype(BF16))
    nw = row2(norm_mix_w[0])
    q_scale = (MLA_QK ** -0.5) * math.log2(math.e)

    tabs = _rope_tables(s + N_META)
    tabs_meta = tuple(t[:N_META] for t in tabs)
    tabs_x = tuple(t[N_META:] for t in tabs)
    rq, rk, rv, sg, q, k, v, gr, gm = _in_proj(x, tabs_x, nw, wts1, q_scale, tm=512)
    meta = meta_tokens.astype(x.dtype)[None]
    _, rk_m, rv_m, _, _, k_m, v_m, _, _ = _in_proj(meta, tabs_meta, nw, wts1, q_scale, tm=N_META)

    lgf = -jnp.exp(ret_decay_fwd[0].astype(F32))
    lgb = -jnp.exp(ret_decay_bwd[0].astype(F32))
    y_ret = _retention(lgf, lgb, rq, rk, rv, sg, rk_m[0], rv_m[0], row2(ret_gn_w[0]), cpb=4)

    y_att = _attention(q, k, v, k_m[0], v_m[0], tq=512, tk=512)

    w_mo = w_mla_out[0].reshape(MLA_HEADS, MLA_DV, d)
    w_mo = jnp.pad(w_mo, ((0, 0), (0, LANES - MLA_DV), (0, 0))).reshape(MLA_HEADS * LANES, d)
    wts4 = (w_ret_out[0].astype(BF16), w_mo.astype(BF16), w_o[0].astype(BF16),
            row2(norm_ffn_w[0]), w_ffn_gate[0].astype(BF16), w_ffn_up[0].astype(BF16),
            w_ffn_down[0].astype(BF16), row2(norm_final_w))
    return _merge_ffn(x, y_ret, y_att, gr, gm, wts4, tm=256)
```

```python
import functools
import math

import jax
import jax.numpy as jnp
from jax import lax
from jax.experimental import pallas as pl
from jax.experimental.pallas import tpu as pltpu

N_META = 16
CHUNK = 128
RET_HEADS = 8
RET_DK = 64
RET_DV = 128
MLA_HEADS = 8
MLA_NOPE = 64
MLA_ROPE = 32
MLA_DV = 64
MLA_QK = MLA_NOPE + MLA_ROPE
ROPE_BASE = 10000.0
RMS_EPS = 1e-6
GN_EPS = 1e-5

LANES = 128
ONES_LANE = MLA_DV
VMEM_LIMIT = 56 * 1024 * 1024

F32 = jnp.float32
BF16 = jnp.bfloat16


def _rms(x, w):
    return x * lax.rsqrt(jnp.mean(x * x, axis=-1, keepdims=True) + RMS_EPS) * w


def _sigmoid(x):
    return 1.0 / (1.0 + jnp.exp(-x))


def _rope_group(x, cos, sin_lo, sin_hi, half):
    return (x * cos + pltpu.roll(x, LANES - half, axis=1) * sin_lo
            + pltpu.roll(x, half, axis=1) * sin_hi)


def _in_proj_kernel(x_ref, nw_ref, w_rq, w_rk, w_rv, w_rg, w_cq, w_ckv, w_kr, w_gr, w_gm,
                    qnw_ref, w_uq, kvnw_ref, w_uk, w_uv,
                    rc_ref, rlo_ref, rhi_ref, mc_ref, mlo_ref, mhi_ref,
                    rq_o, rk_o, rv_o, sg_o, q_o, k_o, v_o, gr_o, gm_o, *, q_scale):
    u = _rms(x_ref[0], nw_ref[...]).astype(BF16)

    def mm(a, w):
        return jnp.dot(a, w[...], preferred_element_type=F32)

    rc, rlo, rhi = rc_ref[...], rlo_ref[...], rhi_ref[...]
    mc, mlo, mhi = mc_ref[...], mlo_ref[...], mhi_ref[...]

    rq = mm(u, w_rq)
    rk = mm(u, w_rk)
    for c in range(rq.shape[1] // LANES):
        sl = slice(c * LANES, (c + 1) * LANES)
        rq_o[0, :, sl] = _rope_group(rq[:, sl], rc, rlo, rhi, RET_DK // 2).astype(BF16)
        rk_o[0, :, sl] = (_rope_group(rk[:, sl], rc, rlo, rhi, RET_DK // 2)
                          * (RET_DK ** -0.5)).astype(BF16)
    rv_o[0] = mm(u, w_rv).astype(BF16)
    rg = mm(u, w_rg)
    sg_o[0] = (rg * _sigmoid(rg)).astype(BF16)
    gr_o[0] = _sigmoid(mm(u, w_gr)).astype(BF16)
    gm_o[0] = _sigmoid(mm(u, w_gm)).astype(BF16)

    cq = _rms(mm(u, w_cq), qnw_ref[...]).astype(BF16)
    q = mm(cq, w_uq)
    ckv = _rms(mm(u, w_ckv), kvnw_ref[...]).astype(BF16)
    kn = mm(ckv, w_uk)
    kr = _rope_group(mm(u, w_kr), mc, mlo, mhi, MLA_ROPE // 2)
    for h in range(MLA_HEADS):
        sl = slice(h * LANES, (h + 1) * LANES)
        q_o[0, :, sl] = (_rope_group(q[:, sl], mc, mlo, mhi, MLA_ROPE // 2) * q_scale).astype(BF16)
        k_o[0, :, sl] = (kn[:, sl] + kr).astype(BF16)
    v = mm(ckv, w_uv)
    lane = lax.broadcasted_iota(jnp.int32, v.shape, 1)
    v_o[0] = jnp.where(lane % LANES == ONES_LANE, 1.0, v).astype(BF16)


def _in_proj(rows, tabs, nw, wts, q_scale, tm):
    nb, r, d = rows.shape
    grid = (r // tm, nb)
    const = lambda i, b: (0, 0)

    def wspec(w):
        return pl.BlockSpec(w.shape, const, pipeline_mode=pl.Buffered(1))

    row_spec = lambda width: pl.BlockSpec((1, tm, width), lambda i, b: (b, i, 0))
    tab_spec = pl.BlockSpec((tm, LANES), lambda i, b: (i, 0))
    (w_rq, w_rk, w_rv, w_rg, w_cq, w_ckv, w_kr, w_gr, w_gm, qnw, w_uq, kvnw, w_uk, w_uv) = wts
    out_widths = (w_rq.shape[1], w_rk.shape[1], w_rv.shape[1], w_rg.shape[1],
                  w_uq.shape[1], w_uk.shape[1], w_uv.shape[1], w_gr.shape[1], w_gm.shape[1])
    return pl.pallas_call(
        functools.partial(_in_proj_kernel, q_scale=q_scale),
        grid=grid,
        in_specs=[row_spec(d), wspec(nw)] + [wspec(w) for w in wts] + [tab_spec] * 6,
        out_specs=[row_spec(w) for w in out_widths],
        out_shape=[jax.ShapeDtypeStruct((nb, r, w), BF16) for w in out_widths],
        compiler_params=pltpu.CompilerParams(
            dimension_semantics=("arbitrary", "arbitrary"), vmem_limit_bytes=VMEM_LIMIT),
        name="in_proj",
    )(rows, nw, *wts, *tabs)


def _retention_kernel(lgf_ref, lgb_ref, q_ref, k_ref, v_ref, sg_ref, km_ref, vm_ref, gnw_ref,
                      o_ref, sb_all, sf_run, sb_run, dmask, wqf, wqb, wkf, wkb, gcf, gcb,
                      *, cpb):
    p = pl.program_id(1)
    n = pl.program_id(2)
    nblk = pl.num_programs(2)
    c_len = CHUNK
    row = lax.broadcasted_iota(jnp.int32, (c_len, LANES), 0).astype(F32)
    col = lax.broadcasted_iota(jnp.int32, (c_len, LANES), 1).astype(F32)
    lane_i = lax.broadcasted_iota(jnp.int32, (c_len, LANES), 1)
    tn = (((0,), (0,)), ((), ()))

    def head_lanes(h):
        lo = (h % 2) * RET_DK
        return (lane_i >= lo) & (lane_i < lo + RET_DK)

    @pl.when((p == 0) & (n == 0))
    def _tables():
        rel = row - col
        for h in range(RET_HEADS):
            lf = lgf_ref[h]
            lb = lgb_ref[h]
            dmask[h] = jnp.exp(jnp.where(rel >= 0, lf * rel, -lb * rel))
            hm = head_lanes(h)
            wqf[h] = jnp.where(hm, jnp.exp(lf * (row + 1.0)), 0.0)
            wqb[h] = jnp.where(hm, jnp.exp(lb * (c_len - row)), 0.0)
            wkf[h] = jnp.exp(lf * (c_len - 1.0 - row))
            wkb[h] = jnp.exp(lb * row)
            gcf[h] = jnp.exp(jnp.full((8, LANES), lf * c_len, F32))
            gcb[h] = jnp.exp(jnp.full((8, LANES), lb * c_len, F32))
            sb_run[h] = jnp.zeros((c_len, RET_DV), F32)

    def k_pair(c, h):
        return k_ref[0, c * c_len:(c + 1) * c_len, (h // 2) * LANES:(h // 2 + 1) * LANES]

    def v_head(c, h):
        return v_ref[0, c * c_len:(c + 1) * c_len, h * RET_DV:(h + 1) * RET_DV]

    @pl.when(p == 0)
    def _backward_states():
        blk = nblk - 1 - n
        for c in range(cpb - 1, -1, -1):
            g = blk * cpb + c
            for h in range(RET_HEADS):
                sb_all[g, h] = sb_run[h].astype(BF16)
                kw = (k_pair(c, h).astype(F32) * wkb[h]).astype(BF16)
                kv = lax.dot_general(kw, v_head(c, h), tn, preferred_element_type=F32)
                sb_run[h] = gcb[h][0:1, :] * sb_run[h] + kv

    @pl.when(p == 1)
    def _forward_outputs():
        @pl.when(n == 0)
        def _meta_state():
            mrow = lax.broadcasted_iota(jnp.int32, (N_META, LANES), 0).astype(F32)
            for h in range(RET_HEADS):
                wm = jnp.exp(lgf_ref[h] * (N_META - 1.0 - mrow))
                kmw = (km_ref[:, (h // 2) * LANES:(h // 2 + 1) * LANES].astype(F32) * wm).astype(BF16)
                vmh = vm_ref[:, h * RET_DV:(h + 1) * RET_DV]
                sf_run[h] = lax.dot_general(kmw, vmh, tn, preferred_element_type=F32)

        for c in range(cpb):
            g = n * cpb + c
            rows = slice(c * c_len, (c + 1) * c_len)
            for h in range(RET_HEADS):
                qp = q_ref[0, rows, (h // 2) * LANES:(h // 2 + 1) * LANES].astype(F32)
                kp = k_pair(c, h)
                vh = v_head(c, h)
                qh = jnp.where(head_lanes(h), qp, 0.0).astype(BF16)
                s = lax.dot_general(qh, kp, (((1,), (1,)), ((), ())), preferred_element_type=F32)
                o = jnp.dot((s * dmask[h]).astype(BF16), vh, preferred_element_type=F32)
                o = o + jnp.dot((qp * wqf[h]).astype(BF16), sf_run[h].astype(BF16),
                                preferred_element_type=F32)
                o = o + jnp.dot((qp * wqb[h]).astype(BF16), sb_all[g, h],
                                preferred_element_type=F32)
                kw = (kp.astype(F32) * wkf[h]).astype(BF16)
                kv = lax.dot_general(kw, vh, tn, preferred_element_type=F32)
                sf_run[h] = gcf[h][0:1, :] * sf_run[h] + kv
                mu = jnp.mean(o, axis=-1, keepdims=True)
                d = o - mu
                var = jnp.mean(d * d, axis=-1, keepdims=True)
                yn = d * lax.rsqrt(var + GN_EPS)
                hs = slice(h * RET_DV, (h + 1) * RET_DV)
                y = yn * gnw_ref[:, hs] * sg_ref[0, rows, hs].astype(F32)
                o_ref[0, rows, hs] = y.astype(BF16)


def _retention(lgf, lgb, rq, rk, rv, sg, km, vm, gnw, cpb):
    b, s, qk_w = rq.shape
    v_w = rv.shape[2]
    rb = cpb * CHUNK
    nblk = s // rb
    n_chunks = s // CHUNK
    q_idx = lambda bi, p, n, *_: (bi, jnp.where(p == 0, 0, n), 0)
    kv_idx = lambda bi, p, n, *_: (bi, jnp.where(p == 0, nblk - 1 - n, n), 0)
    const = lambda bi, p, n, *_: (0, 0)
    tab = lambda rows_: pltpu.VMEM((RET_HEADS, rows_, LANES), F32)
    grid_spec = pltpu.PrefetchScalarGridSpec(
        num_scalar_prefetch=2,
        grid=(b, 2, nblk),
        in_specs=[
            pl.BlockSpec((1, rb, qk_w), q_idx),
            pl.BlockSpec((1, rb, qk_w), kv_idx),
            pl.BlockSpec((1, rb, v_w), kv_idx),
            pl.BlockSpec((1, rb, v_w), q_idx),
            pl.BlockSpec(km.shape, const),
            pl.BlockSpec(vm.shape, const),
            pl.BlockSpec(gnw.shape, const),
        ],
        out_specs=pl.BlockSpec((1, rb, v_w), q_idx),
        scratch_shapes=[
            pltpu.VMEM((n_chunks, RET_HEADS, LANES, RET_DV), BF16),
            tab(LANES), tab(LANES),
            tab(CHUNK), tab(CHUNK), tab(CHUNK), tab(CHUNK), tab(CHUNK),
            tab(8), tab(8),
        ],
    )
    return pl.pallas_call(
        functools.partial(_retention_kernel, cpb=cpb),
        grid_spec=grid_spec,
        out_shape=jax.ShapeDtypeStruct((b, s, v_w), BF16),
        compiler_params=pltpu.CompilerParams(
            dimension_semantics=("arbitrary", "arbitrary", "arbitrary"),
            vmem_limit_bytes=VMEM_LIMIT),
        name="retention",
    )(lgf, lgb, rq, rk, rv, sg, km, vm, gnw)


def _attn_kernel(q_ref, k_ref, v_ref, km_ref, vm_ref, o_ref, *, tk, unroll):
    nt = (((1,), (1,)), ((), ()))
    q = q_ref[0]

    s0 = lax.dot_general(q, km_ref[...], nt, preferred_element_type=F32)
    m0 = jnp.max(s0, axis=1, keepdims=True)
    acc0 = jnp.dot(jnp.exp2(s0 - m0).astype(BF16), vm_ref[...], preferred_element_type=F32)

    def body(j, carry):
        m, acc = carry
        start = pl.multiple_of(j * tk, tk)
        kt = k_ref[0, pl.ds(start, tk), :]
        vt = v_ref[0, pl.ds(start, tk), :]
        s = lax.dot_general(q, kt, nt, preferred_element_type=F32)
        m_new = jnp.maximum(m, jnp.max(s, axis=1, keepdims=True))
        p = jnp.exp2(s - m_new).astype(BF16)
        acc = jnp.exp2(m - m_new) * acc + jnp.dot(p, vt, preferred_element_type=F32)
        return m_new, acc

    _, acc = lax.fori_loop(0, k_ref.shape[1] // tk, body, (m0, acc0), unroll=unroll)
    o_ref[0] = (acc / acc[:, ONES_LANE:ONES_LANE + 1]).astype(BF16)


def _attention(q, k, v, km, vm, tq, tk, unroll):
    b, s, w = q.shape
    heads = w // LANES
    return pl.pallas_call(
        functools.partial(_attn_kernel, tk=tk, unroll=unroll),
        grid=(b, heads, s // tq),
        in_specs=[
            pl.BlockSpec((1, tq, LANES), lambda bi, h, i: (bi, i, h)),
            pl.BlockSpec((1, s, LANES), lambda bi, h, i: (bi, 0, h)),
            pl.BlockSpec((1, s, LANES), lambda bi, h, i: (bi, 0, h)),
            pl.BlockSpec((N_META, LANES), lambda bi, h, i: (0, h)),
            pl.BlockSpec((N_META, LANES), lambda bi, h, i: (0, h)),
        ],
        out_specs=pl.BlockSpec((1, tq, LANES), lambda bi, h, i: (bi, i, h)),
        out_shape=jax.ShapeDtypeStruct((b, s, w), BF16),
        compiler_params=pltpu.CompilerParams(
            dimension_semantics=("arbitrary", "arbitrary", "arbitrary"),
            vmem_limit_bytes=VMEM_LIMIT),
        name="mla_attention",
    )(q, k, v, km, vm)


def _merge_ffn_kernel(x_ref, yr_ref, ya_ref, gr_ref, gm_ref, w_ro, w_mo, w_o, nfw_ref,
                      w_g, w_u, w_d, nfin_ref, o_ref):
    def mm(a, w):
        return jnp.dot(a, w[...], preferred_element_type=F32)

    merged = (gr_ref[0].astype(F32) * mm(yr_ref[0], w_ro)
              + gm_ref[0].astype(F32) * mm(ya_ref[0], w_mo))
    h1 = x_ref[0] + mm(merged.astype(BF16), w_o)
    u = _rms(h1, nfw_ref[...]).astype(BF16)
    g = mm(u, w_g)
    act = (g * _sigmoid(g) * mm(u, w_u)).astype(BF16)
    h2 = h1 + mm(act, w_d)
    o_ref[0] = _rms(h2, nfin_ref[...])


def _merge_ffn(x, yr, ya, gr, gm, wts, tm):
    b, s, d = x.shape
    const = lambda bi, i: (0, 0)
    wspec = lambda w: pl.BlockSpec(w.shape, const, pipeline_mode=pl.Buffered(1))
    row = pl.BlockSpec((1, tm, d), lambda bi, i: (bi, i, 0))
    return pl.pallas_call(
        _merge_ffn_kernel,
        grid=(b, s // tm),
        in_specs=[row] * 5 + [wspec(w) for w in wts],
        out_specs=row,
        out_shape=jax.ShapeDtypeStruct((b, s, d), F32),
        compiler_params=pltpu.CompilerParams(
            dimension_semantics=("arbitrary", "arbitrary"), vmem_limit_bytes=VMEM_LIMIT),
        name="merge_ffn",
    )(x, yr, ya, gr, gm, *wts)


def _rope_tables(n_pos):
    pos = jnp.arange(n_pos, dtype=F32)[:, None]
    lane = jnp.arange(LANES)

    def angles(half):
        inv = ROPE_BASE ** (-jnp.arange(half, dtype=F32) / half)
        return pos * inv[None, :]

    half = RET_DK // 2
    ang = angles(half)[:, lane % half]
    first = (lane % RET_DK) < half
    r_cos = jnp.cos(ang)
    r_lo = jnp.where(first, -jnp.sin(ang), 0.0)
    r_hi = jnp.where(first, 0.0, jnp.sin(ang))
    half = MLA_ROPE // 2
    ang = angles(half)[:, lane % half]
    in_rope = (lane >= MLA_NOPE) & (lane < MLA_QK)
    first = in_rope & (lane < MLA_NOPE + half)
    second = in_rope & (lane >= MLA_NOPE + half)
    m_cos = jnp.where(in_rope, jnp.cos(ang), 1.0)
    m_lo = jnp.where(first, -jnp.sin(ang), 0.0)
    m_hi = jnp.where(second, jnp.sin(ang), 0.0)
    return r_cos, r_lo, r_hi, m_cos, m_lo, m_hi


def _pad_heads(w, heads, width):
    k = w.shape[0]
    w = w.reshape(k, heads, width)
    return jnp.pad(w, ((0, 0), (0, 0), (0, LANES - width))).reshape(k, heads * LANES)


def kernel(x, meta_tokens, norm_mix_w, w_in, ret_decay_fwd, ret_decay_bwd, ret_gn_w, w_ret_out, mla_q_norm_w, w_uq, mla_kv_norm_w, w_uk, w_uv, w_mla_out, w_o, norm_ffn_w, w_ffn_gate, w_ffn_up, w_ffn_down, norm_final_w):
    b, s, d = x.shape
    assert w_in.shape[0] == 1, "one layer: the meta rows are dropped after it"
    ret_qk_w = RET_HEADS * RET_DK
    ret_v_w = RET_HEADS * RET_DV
    q_rank = w_uq.shape[1]
    kv_rank = w_uk.shape[1]
    sizes = (ret_qk_w, ret_qk_w, ret_v_w, ret_v_w, q_rank, kv_rank, MLA_ROPE, d, d)
    assert sum(sizes) == w_in.shape[2]
    bounds = [0]
    for sz in sizes:
        bounds.append(bounds[-1] + sz)
    w_rq, w_rk, w_rv, w_rg, w_cq, w_ckv, w_kr, w_gr, w_gm = (
        w_in[0][:, lo:hi].astype(BF16) for lo, hi in zip(bounds[:-1], bounds[1:]))
    w_kr = jnp.pad(w_kr, ((0, 0), (MLA_NOPE, LANES - MLA_QK)))
    row2 = lambda a: a.reshape(1, -1).astype(F32)
    wts1 = (w_rq, w_rk, w_rv, w_rg, w_cq, w_ckv, w_kr, w_gr, w_gm,
            row2(mla_q_norm_w[0]), _pad_heads(w_uq[0], MLA_HEADS, MLA_QK).astype(BF16),
            row2(mla_kv_norm_w[0]), _pad_heads(w_uk[0], MLA_HEADS, MLA_NOPE).astype(BF16),
            _pad_heads(w_uv[0], MLA_HEADS, MLA_DV).astype(BF16))
    nw = row2(norm_mix_w[0])
    q_scale = (MLA_QK ** -0.5) * math.log2(math.e)

    tabs = _rope_tables(s + N_META)
    tabs_meta = tuple(t[:N_META] for t in tabs)
    tabs_x = tuple(t[N_META:] for t in tabs)
    rq, rk, rv, sg, q, k, v, gr, gm = _in_proj(x, tabs_x, nw, wts1, q_scale, tm=512)
    meta = meta_tokens.astype(x.dtype)[None]
    _, rk_m, rv_m, _, _, k_m, v_m, _, _ = _in_proj(meta, tabs_meta, nw, wts1, q_scale, tm=N_META)

    lgf = -jnp.exp(ret_decay_fwd[0].astype(F32))
    lgb = -jnp.exp(ret_decay_bwd[0].astype(F32))
    y_ret = _retention(lgf, lgb, rq, rk, rv, sg, rk_m[0], rv_m[0], row2(ret_gn_w[0]), cpb=4)

    y_att = _attention(q, k, v, k_m[0], v_m[0], tq=1024, tk=512, unroll=8)

    w_mo = w_mla_out[0].reshape(MLA_HEADS, MLA_DV, d)
    w_mo = jnp.pad(w_mo, ((0, 0), (0, LANES - MLA_DV), (0, 0))).reshape(MLA_HEADS * LANES, d)
    wts4 = (w_ret_out[0].astype(BF16), w_mo.astype(BF16), w_o[0].astype(BF16),
            row2(norm_ffn_w[0]), w_ffn_gate[0].astype(BF16), w_ffn_up[0].astype(BF16),
            w_ffn_down[0].astype(BF16), row2(norm_final_w))
    return _merge_ffn(x, y_ret, y_att, gr, gm, wts4, tm=256)
```

```python
import functools
import math

import jax
import jax.numpy as jnp
import numpy as np
from jax import lax
from jax.experimental import pallas as pl
from jax.experimental.pallas import tpu as pltpu

N_META = 16
CHUNK = 128
RET_HEADS = 8
RET_DK = 64
RET_DV = 128
MLA_HEADS = 8
MLA_NOPE = 64
MLA_ROPE = 32
MLA_DV = 64
MLA_QK = MLA_NOPE + MLA_ROPE
ROPE_BASE = 10000.0
RMS_EPS = 1e-6
GN_EPS = 1e-5

LANES = 128
ONES_LANE = MLA_DV
VMEM_LIMIT = 56 * 1024 * 1024

F32 = jnp.float32
BF16 = jnp.bfloat16


def _rms(x, w):
    return x * lax.rsqrt(jnp.mean(x * x, axis=-1, keepdims=True) + RMS_EPS) * w


def _sigmoid(x):
    return 1.0 / (1.0 + jnp.exp(-x))


def _rope_group(x, cos, sin_lo, sin_hi, half):
    return (x * cos + pltpu.roll(x, LANES - half, axis=1) * sin_lo
            + pltpu.roll(x, half, axis=1) * sin_hi)


def _in_proj_kernel(x_ref, nw_ref, w_rq, w_rk, w_rv, w_rg, w_lat, w_gr, w_gm,
                    qnw_ref, w_uq, kvnw_ref, w_uk, w_uv,
                    rc_ref, rlo_ref, rhi_ref, mc_ref, mlo_ref, mhi_ref,
                    rq_o, rk_o, rv_o, sg_o, q_o, k_o, v_o, gr_o, gm_o, *, q_scale):
    u = _rms(x_ref[0], nw_ref[...]).astype(BF16)

    def mm(a, w):
        return jnp.dot(a, w[...], preferred_element_type=F32)

    rc, rlo, rhi = rc_ref[...], rlo_ref[...], rhi_ref[...]
    mc, mlo, mhi = mc_ref[...], mlo_ref[...], mhi_ref[...]

    rq = mm(u, w_rq)
    rk = mm(u, w_rk)
    for c in range(rq.shape[1] // LANES):
        sl = slice(c * LANES, (c + 1) * LANES)
        rq_o[0, :, sl] = _rope_group(rq[:, sl], rc, rlo, rhi, RET_DK // 2).astype(BF16)
        rk_o[0, :, sl] = (_rope_group(rk[:, sl], rc, rlo, rhi, RET_DK // 2)
                          * (RET_DK ** -0.5)).astype(BF16)
    rv_o[0] = mm(u, w_rv).astype(BF16)
    rg = mm(u, w_rg)
    sg_o[0] = (rg * _sigmoid(rg)).astype(BF16)
    gr_o[0] = _sigmoid(mm(u, w_gr)).astype(BF16)
    gm_o[0] = _sigmoid(mm(u, w_gm)).astype(BF16)

    lat = mm(u, w_lat)
    q_rank = qnw_ref.shape[1]
    kv_rank = kvnw_ref.shape[1]
    cq = _rms(lat[:, :q_rank], qnw_ref[...]).astype(BF16)
    q = mm(cq, w_uq)
    ckv = _rms(lat[:, q_rank:q_rank + kv_rank], kvnw_ref[...]).astype(BF16)
    kn = mm(ckv, w_uk)
    kr = _rope_group(lat[:, q_rank + kv_rank:], mc, mlo, mhi, MLA_ROPE // 2)
    for h in range(MLA_HEADS):
        sl = slice(h * LANES, (h + 1) * LANES)
        q_o[0, :, sl] = (_rope_group(q[:, sl], mc, mlo, mhi, MLA_ROPE // 2) * q_scale).astype(BF16)
        k_o[0, :, sl] = (kn[:, sl] + kr).astype(BF16)
    v = mm(ckv, w_uv)
    lane = lax.broadcasted_iota(jnp.int32, v.shape, 1)
    v_o[0] = jnp.where(lane % LANES == ONES_LANE, 1.0, v).astype(BF16)


def _in_proj(rows, tabs, nw, wts, q_scale, tm):
    nb, r, d = rows.shape
    grid = (r // tm, nb)
    const = lambda i, b: (0, 0)

    def wspec(w):
        return pl.BlockSpec(w.shape, const, pipeline_mode=pl.Buffered(1))

    row_spec = lambda width: pl.BlockSpec((1, tm, width), lambda i, b: (b, i, 0))
    tab_spec = pl.BlockSpec((tm, LANES), lambda i, b: (i, 0))
    (w_rq, w_rk, w_rv, w_rg, w_lat, w_gr, w_gm, qnw, w_uq, kvnw, w_uk, w_uv) = wts
    out_widths = (w_rq.shape[1], w_rk.shape[1], w_rv.shape[1], w_rg.shape[1],
                  w_uq.shape[1], w_uk.shape[1], w_uv.shape[1], w_gr.shape[1], w_gm.shape[1])
    return pl.pallas_call(
        functools.partial(_in_proj_kernel, q_scale=q_scale),
        grid=grid,
        in_specs=[row_spec(d), wspec(nw)] + [wspec(w) for w in wts] + [tab_spec] * 6,
        out_specs=[row_spec(w) for w in out_widths],
        out_shape=[jax.ShapeDtypeStruct((nb, r, w), BF16) for w in out_widths],
        compiler_params=pltpu.CompilerParams(
            dimension_semantics=("arbitrary", "arbitrary"), vmem_limit_bytes=VMEM_LIMIT),
        name="in_proj",
    )(rows, nw, *wts, *tabs)


def _retention_kernel(lgf_ref, lgb_ref, q_ref, k_ref, v_ref, sg_ref, km_ref, vm_ref, gnw_ref,
                      o_ref, sb_all, sf_run, sb_run, dmask, wqf, wqb, wkf, wkb, gcf, gcb,
                      *, cpb):
    p = pl.program_id(1)
    n = pl.program_id(2)
    nblk = pl.num_programs(2)
    c_len = CHUNK
    row = lax.broadcasted_iota(jnp.int32, (c_len, LANES), 0).astype(F32)
    col = lax.broadcasted_iota(jnp.int32, (c_len, LANES), 1).astype(F32)
    lane_i = lax.broadcasted_iota(jnp.int32, (c_len, LANES), 1)
    tn = (((0,), (0,)), ((), ()))

    def head_lanes(h):
        lo = (h % 2) * RET_DK
        return (lane_i >= lo) & (lane_i < lo + RET_DK)

    @pl.when((p == 0) & (n == 0))
    def _tables():
        rel = row - col
        for h in range(RET_HEADS):
            lf = lgf_ref[h]
            lb = lgb_ref[h]
            dmask[h] = jnp.exp(jnp.where(rel >= 0, lf * rel, -lb * rel))
            hm = head_lanes(h)
            wqf[h] = jnp.where(hm, jnp.exp(lf * (row + 1.0)), 0.0)
            wqb[h] = jnp.where(hm, jnp.exp(lb * (c_len - row)), 0.0)
            wkf[h] = jnp.exp(lf * (c_len - 1.0 - row))
            wkb[h] = jnp.exp(lb * row)
            gcf[h] = jnp.exp(jnp.full((8, LANES), lf * c_len, F32))
            gcb[h] = jnp.exp(jnp.full((8, LANES), lb * c_len, F32))
            sb_run[h] = jnp.zeros((c_len, RET_DV), F32)

    def k_pair(c, h):
        return k_ref[0, c * c_len:(c + 1) * c_len, (h // 2) * LANES:(h // 2 + 1) * LANES]

    def v_head(c, h):
        return v_ref[0, c * c_len:(c + 1) * c_len, h * RET_DV:(h + 1) * RET_DV]

    @pl.when(p == 0)
    def _backward_states():
        blk = nblk - 1 - n
        for c in range(cpb - 1, -1, -1):
            g = blk * cpb + c
            for h in range(RET_HEADS):
                sb_all[g, h] = sb_run[h].astype(BF16)
                kw = (k_pair(c, h).astype(F32) * wkb[h]).astype(BF16)
                kv = lax.dot_general(kw, v_head(c, h), tn, preferred_element_type=F32)
                sb_run[h] = gcb[h][0:1, :] * sb_run[h] + kv

    @pl.when(p == 1)
    def _forward_outputs():
        @pl.when(n == 0)
        def _meta_state():
            mrow = lax.broadcasted_iota(jnp.int32, (N_META, LANES), 0).astype(F32)
            for h in range(RET_HEADS):
                wm = jnp.exp(lgf_ref[h] * (N_META - 1.0 - mrow))
                kmw = (km_ref[:, (h // 2) * LANES:(h // 2 + 1) * LANES].astype(F32) * wm).astype(BF16)
                vmh = vm_ref[:, h * RET_DV:(h + 1) * RET_DV]
                sf_run[h] = lax.dot_general(kmw, vmh, tn, preferred_element_type=F32)

        for c in range(cpb):
            g = n * cpb + c
            rows = slice(c * c_len, (c + 1) * c_len)
            for h in range(RET_HEADS):
                qp = q_ref[0, rows, (h // 2) * LANES:(h // 2 + 1) * LANES].astype(F32)
                kp = k_pair(c, h)
                vh = v_head(c, h)
                qh = jnp.where(head_lanes(h), qp, 0.0).astype(BF16)
                s = lax.dot_general(qh, kp, (((1,), (1,)), ((), ())), preferred_element_type=F32)
                o = jnp.dot((s * dmask[h]).astype(BF16), vh, preferred_element_type=F32)
                o = o + jnp.dot((qp * wqf[h]).astype(BF16), sf_run[h].astype(BF16),
                                preferred_element_type=F32)
                o = o + jnp.dot((qp * wqb[h]).astype(BF16), sb_all[g, h],
                                preferred_element_type=F32)
                kw = (kp.astype(F32) * wkf[h]).astype(BF16)
                kv = lax.dot_general(kw, vh, tn, preferred_element_type=F32)
                sf_run[h] = gcf[h][0:1, :] * sf_run[h] + kv
                mu = jnp.mean(o, axis=-1, keepdims=True)
                d = o - mu
                var = jnp.mean(d * d, axis=-1, keepdims=True)
                yn = d * lax.rsqrt(var + GN_EPS)
                hs = slice(h * RET_DV, (h + 1) * RET_DV)
                y = yn * gnw_ref[:, hs] * sg_ref[0, rows, hs].astype(F32)
                o_ref[0, rows, hs] = y.astype(BF16)


def _retention(lgf, lgb, rq, rk, rv, sg, km, vm, gnw, cpb):
    b, s, qk_w = rq.shape
    v_w = rv.shape[2]
    rb = cpb * CHUNK
    nblk = s // rb
    n_chunks = s // CHUNK
    q_idx = lambda bi, p, n, *_: (bi, jnp.where(p == 0, 0, n), 0)
    kv_idx = lambda bi, p, n, *_: (bi, jnp.where(p == 0, nblk - 1 - n, n), 0)
    const = lambda bi, p, n, *_: (0, 0)
    tab = lambda rows_: pltpu.VMEM((RET_HEADS, rows_, LANES), F32)
    grid_spec = pltpu.PrefetchScalarGridSpec(
        num_scalar_prefetch=2,
        grid=(b, 2, nblk),
        in_specs=[
            pl.BlockSpec((1, rb, qk_w), q_idx),
            pl.BlockSpec((1, rb, qk_w), kv_idx),
            pl.BlockSpec((1, rb, v_w), kv_idx),
            pl.BlockSpec((1, rb, v_w), q_idx),
            pl.BlockSpec(km.shape, const),
            pl.BlockSpec(vm.shape, const),
            pl.BlockSpec(gnw.shape, const),
        ],
        out_specs=pl.BlockSpec((1, rb, v_w), q_idx),
        scratch_shapes=[
            pltpu.VMEM((n_chunks, RET_HEADS, LANES, RET_DV), BF16),
            tab(LANES), tab(LANES),
            tab(CHUNK), tab(CHUNK), tab(CHUNK), tab(CHUNK), tab(CHUNK),
            tab(8), tab(8),
        ],
    )
    return pl.pallas_call(
        functools.partial(_retention_kernel, cpb=cpb),
        grid_spec=grid_spec,
        out_shape=jax.ShapeDtypeStruct((b, s, v_w), BF16),
        compiler_params=pltpu.CompilerParams(
            dimension_semantics=("arbitrary", "arbitrary", "arbitrary"),
            vmem_limit_bytes=VMEM_LIMIT),
        name="retention",
    )(lgf, lgb, rq, rk, rv, sg, km, vm, gnw)


def _attn_kernel(q_ref, k_ref, v_ref, km_ref, vm_ref, o_ref, *, tk):
    nt = (((1,), (1,)), ((), ()))
    q = q_ref[0]

    s0 = lax.dot_general(q, km_ref[...], nt, preferred_element_type=F32)
    m0 = jnp.max(s0, axis=1, keepdims=True)
    acc0 = jnp.dot(jnp.exp2(s0 - m0).astype(BF16), vm_ref[...], preferred_element_type=F32)

    m, acc = m0, acc0
    for j in range(k_ref.shape[1] // tk):
        kt = k_ref[0, j * tk:(j + 1) * tk, :]
        vt = v_ref[0, j * tk:(j + 1) * tk, :]
        s = lax.dot_general(q, kt, nt, preferred_element_type=F32)
        m_new = jnp.maximum(m, jnp.max(s, axis=1, keepdims=True))
        p = jnp.exp2(s - m_new).astype(BF16)
        acc = jnp.exp2(m - m_new) * acc + jnp.dot(p, vt, preferred_element_type=F32)
        m = m_new
    o_ref[0] = (acc / acc[:, ONES_LANE:ONES_LANE + 1]).astype(BF16)


def _attention(q, k, v, km, vm, tq, tk):
    b, s, w = q.shape
    heads = w // LANES
    return pl.pallas_call(
        functools.partial(_attn_kernel, tk=tk),
        grid=(b, heads, s // tq),
        in_specs=[
            pl.BlockSpec((1, tq, LANES), lambda bi, h, i: (bi, i, h)),
            pl.BlockSpec((1, s, LANES), lambda bi, h, i: (bi, 0, h)),
            pl.BlockSpec((1, s, LANES), lambda bi, h, i: (bi, 0, h)),
            pl.BlockSpec((N_META, LANES), lambda bi, h, i: (0, h)),
            pl.BlockSpec((N_META, LANES), lambda bi, h, i: (0, h)),
        ],
        out_specs=pl.BlockSpec((1, tq, LANES), lambda bi, h, i: (bi, i, h)),
        out_shape=jax.ShapeDtypeStruct((b, s, w), BF16),
        compiler_params=pltpu.CompilerParams(
            dimension_semantics=("arbitrary", "arbitrary", "arbitrary"),
            vmem_limit_bytes=VMEM_LIMIT),
        name="mla_attention",
    )(q, k, v, km, vm)


def _merge_ffn_kernel(x_ref, yr_ref, ya_ref, gr_ref, gm_ref, w_ro, w_mo, w_o, nfw_ref,
                      w_g, w_u, w_d, nfin_ref, o_ref):
    def mm(a, w):
        return jnp.dot(a, w[...], preferred_element_type=F32)

    lane = lax.broadcasted_iota(jnp.int32, (ya_ref.shape[1], LANES), 1)
    packed = []
    for i in range(MLA_HEADS // 2):
        even = ya_ref[0, :, (2 * i) * LANES:(2 * i + 1) * LANES].astype(F32)
        odd = ya_ref[0, :, (2 * i + 1) * LANES:(2 * i + 2) * LANES].astype(F32)
        packed.append(jnp.where(lane < MLA_DV, even, pltpu.roll(odd, MLA_DV, axis=1)))
    ya = jnp.concatenate(packed, axis=1).astype(BF16)
    merged = (gr_ref[0].astype(F32) * mm(yr_ref[0], w_ro)
              + gm_ref[0].astype(F32) * mm(ya, w_mo))
    h1 = x_ref[0] + mm(merged.astype(BF16), w_o)
    u = _rms(h1, nfw_ref[...]).astype(BF16)
    g = mm(u, w_g)
    act = (g * _sigmoid(g) * mm(u, w_u)).astype(BF16)
    h2 = h1 + mm(act, w_d)
    o_ref[0] = _rms(h2, nfin_ref[...])


def _merge_ffn(x, yr, ya, gr, gm, wts, tm):
    b, s, d = x.shape
    const = lambda bi, i: (0, 0)
    wspec = lambda w: pl.BlockSpec(w.shape, const, pipeline_mode=pl.Buffered(1))
    row = pl.BlockSpec((1, tm, d), lambda bi, i: (bi, i, 0))
    return pl.pallas_call(
        _merge_ffn_kernel,
        grid=(b, s // tm),
        in_specs=[row] * 5 + [wspec(w) for w in wts],
        out_specs=row,
        out_shape=jax.ShapeDtypeStruct((b, s, d), F32),
        compiler_params=pltpu.CompilerParams(
            dimension_semantics=("arbitrary", "arbitrary"), vmem_limit_bytes=VMEM_LIMIT),
        name="merge_ffn",
    )(x, yr, ya, gr, gm, *wts)


def _rope_tables(n_pos):
    pos = np.arange(n_pos, dtype=np.float64)[:, None]
    lane = np.arange(LANES)

    def angles(half):
        inv = ROPE_BASE ** (-np.arange(half, dtype=np.float64) / half)
        return pos * inv[None, :]

    half = RET_DK // 2
    ang = angles(half)[:, lane % half]
    first = (lane % RET_DK) < half
    r_cos = np.cos(ang)
    r_lo = np.where(first, -np.sin(ang), 0.0)
    r_hi = np.where(first, 0.0, np.sin(ang))
    half = MLA_ROPE // 2
    ang = angles(half)[:, lane % half]
    in_rope = (lane >= MLA_NOPE) & (lane < MLA_QK)
    first = in_rope & (lane < MLA_NOPE + half)
    second = in_rope & (lane >= MLA_NOPE + half)
    m_cos = np.where(in_rope, np.cos(ang), 1.0)
    m_lo = np.where(first, -np.sin(ang), 0.0)
    m_hi = np.where(second, np.sin(ang), 0.0)
    return tuple(t.astype(np.float32) for t in (r_cos, r_lo, r_hi, m_cos, m_lo, m_hi))


def _pad_heads(w, heads, width):
    k = w.shape[0]
    w = w.reshape(k, heads, width)
    return jnp.pad(w, ((0, 0), (0, 0), (0, LANES - width))).reshape(k, heads * LANES)


def kernel(x, meta_tokens, norm_mix_w, w_in, ret_decay_fwd, ret_decay_bwd, ret_gn_w, w_ret_out, mla_q_norm_w, w_uq, mla_kv_norm_w, w_uk, w_uv, w_mla_out, w_o, norm_ffn_w, w_ffn_gate, w_ffn_up, w_ffn_down, norm_final_w):
    b, s, d = x.shape
    assert w_in.shape[0] == 1, "one layer: the meta rows are dropped after it"
    ret_qk_w = RET_HEADS * RET_DK
    ret_v_w = RET_HEADS * RET_DV
    q_rank = w_uq.shape[1]
    kv_rank = w_uk.shape[1]
    sizes = (ret_qk_w, ret_qk_w, ret_v_w, ret_v_w, q_rank, kv_rank, MLA_ROPE, d, d)
    assert sum(sizes) == w_in.shape[2]
    bounds = [0]
    for sz in sizes:
        bounds.append(bounds[-1] + sz)
    w_rq, w_rk, w_rv, w_rg, w_cq, w_ckv, w_kr, w_gr, w_gm = (
        w_in[0][:, lo:hi].astype(BF16) for lo, hi in zip(bounds[:-1], bounds[1:]))
    w_lat = jnp.concatenate(
        [w_cq, w_ckv, jnp.pad(w_kr, ((0, 0), (MLA_NOPE, LANES - MLA_QK)))], axis=1)
    row2 = lambda a: a.reshape(1, -1).astype(F32)
    wts1 = (w_rq, w_rk, w_rv, w_rg, w_lat, w_gr, w_gm,
            row2(mla_q_norm_w[0]), _pad_heads(w_uq[0], MLA_HEADS, MLA_QK).astype(BF16),
            row2(mla_kv_norm_w[0]), _pad_heads(w_uk[0], MLA_HEADS, MLA_NOPE).astype(BF16),
            _pad_heads(w_uv[0], MLA_HEADS, MLA_DV).astype(BF16))
    nw = row2(norm_mix_w[0])
    q_scale = (MLA_QK ** -0.5) * math.log2(math.e)

    tabs = _rope_tables(s + N_META)
    tabs_meta = tuple(t[:N_META] for t in tabs)
    tabs_x = tuple(t[N_META:] for t in tabs)
    rq, rk, rv, sg, q, k, v, gr, gm = _in_proj(x, tabs_x, nw, wts1, q_scale, tm=512)
    meta = meta_tokens.astype(x.dtype)[None]
    _, rk_m, rv_m, _, _, k_m, v_m, _, _ = _in_proj(meta, tabs_meta, nw, wts1, q_scale, tm=N_META)

    lgf = -jnp.exp(ret_decay_fwd[0].astype(F32))
    lgb = -jnp.exp(ret_decay_bwd[0].astype(F32))
    y_ret = _retention(lgf, lgb, rq, rk, rv, sg, rk_m[0], rv_m[0], row2(ret_gn_w[0]), cpb=4)

    y_att = _attention(q, k, v, k_m[0], v_m[0], tq=1024, tk=512)

    wts4 = (w_ret_out[0].astype(BF16), w_mla_out[0].astype(BF16), w_o[0].astype(BF16),
            row2(norm_ffn_w[0]), w_ffn_gate[0].astype(BF16), w_ffn_up[0].astype(BF16),
            w_ffn_down[0].astype(BF16), row2(norm_final_w))
    return _merge_ffn(x, y_ret, y_att, gr, gm, wts4, tm=256)
```

```python
import functools
import math

import jax
import jax.numpy as jnp
import numpy as np
from jax import lax
from jax.experimental import pallas as pl
from jax.experimental.pallas import tpu as pltpu

N_META = 16
CHUNK = 128
RET_HEADS = 8
RET_DK = 64
RET_DV = 128
MLA_HEADS = 8
MLA_NOPE = 64
MLA_ROPE = 32
MLA_DV = 64
MLA_QK = MLA_NOPE + MLA_ROPE
ROPE_BASE = 10000.0
RMS_EPS = 1e-6
GN_EPS = 1e-5

LANES = 128
MXU_TILE = 256
ONES_LANE = MLA_DV
VMEM_LIMIT = 56 * 1024 * 1024

F32 = jnp.float32
BF16 = jnp.bfloat16


def _rms(x, w):
    return x * lax.rsqrt(jnp.mean(x * x, axis=-1, keepdims=True) + RMS_EPS) * w


def _sigmoid(x):
    return 1.0 / (1.0 + jnp.exp(-x))


def _rope_group(x, cos, sin_lo, sin_hi, half):
    return (x * cos + pltpu.roll(x, LANES - half, axis=1) * sin_lo
            + pltpu.roll(x, half, axis=1) * sin_hi)


def _in_proj_kernel(x_ref, nw_ref, w_rq, w_rk, w_rv, w_rg, w_lat, w_gr, w_gm,
                    qnw_ref, w_uq, kvnw_ref, w_uk, w_uv,
                    rc_ref, rlo_ref, rhi_ref, mc_ref, mlo_ref, mhi_ref,
                    rq_o, rk_o, rv_o, sg_o, q_o, k_o, v_o, gr_o, gm_o, *, q_scale):
    u = _rms(x_ref[0], nw_ref[...]).astype(BF16)

    def mm(a, w):
        return jnp.dot(a, w[...], preferred_element_type=F32)

    rc, rlo, rhi = rc_ref[...], rlo_ref[...], rhi_ref[...]
    mc, mlo, mhi = mc_ref[...], mlo_ref[...], mhi_ref[...]

    lat = mm(u, w_lat)
    q_rank = qnw_ref.shape[1]
    kv_rank = kvnw_ref.shape[1]
    cq = _rms(lat[:, :q_rank], qnw_ref[...]).astype(BF16)
    ckv = _rms(lat[:, q_rank:q_rank + kv_rank], kvnw_ref[...]).astype(BF16)
    kr = _rope_group(lat[:, q_rank + kv_rank:], mc, mlo, mhi, MLA_ROPE // 2)

    rq = mm(u, w_rq)
    rk = mm(u, w_rk)
    for c in range(rq.shape[1] // LANES):
        sl = slice(c * LANES, (c + 1) * LANES)
        rq_o[0, :, sl] = _rope_group(rq[:, sl], rc, rlo, rhi, RET_DK // 2).astype(BF16)
        rk_o[0, :, sl] = (_rope_group(rk[:, sl], rc, rlo, rhi, RET_DK // 2)
                          * (RET_DK ** -0.5)).astype(BF16)

    q = mm(cq, w_uq)
    kn = mm(ckv, w_uk)
    for h in range(MLA_HEADS):
        sl = slice(h * LANES, (h + 1) * LANES)
        q_o[0, :, sl] = (_rope_group(q[:, sl], mc, mlo, mhi, MLA_ROPE // 2) * q_scale).astype(BF16)
        k_o[0, :, sl] = (kn[:, sl] + kr).astype(BF16)
    v = mm(ckv, w_uv)
    lane = lax.broadcasted_iota(jnp.int32, v.shape, 1)
    v_o[0] = jnp.where(lane % LANES == ONES_LANE, 1.0, v).astype(BF16)

    rv_o[0] = mm(u, w_rv).astype(BF16)
    rg = mm(u, w_rg)
    sg_o[0] = (rg * _sigmoid(rg)).astype(BF16)
    gr_o[0] = _sigmoid(mm(u, w_gr)).astype(BF16)
    gm_o[0] = _sigmoid(mm(u, w_gm)).astype(BF16)


def _in_proj(rows, tabs, nw, wts, q_scale, tm):
    nb, r, d = rows.shape
    grid = (r // tm, nb)
    const = lambda i, b: (0, 0)

    def wspec(w):
        return pl.BlockSpec(w.shape, const, pipeline_mode=pl.Buffered(1))

    row_spec = lambda width: pl.BlockSpec((1, tm, width), lambda i, b: (b, i, 0))
    tab_spec = pl.BlockSpec((tm, LANES), lambda i, b: (i, 0))
    (w_rq, w_rk, w_rv, w_rg, w_lat, w_gr, w_gm, qnw, w_uq, kvnw, w_uk, w_uv) = wts
    out_widths = (w_rq.shape[1], w_rk.shape[1], w_rv.shape[1], w_rg.shape[1],
                  w_uq.shape[1], w_uk.shape[1], w_uv.shape[1], w_gr.shape[1], w_gm.shape[1])
    return pl.pallas_call(
        functools.partial(_in_proj_kernel, q_scale=q_scale),
        grid=grid,
        in_specs=[row_spec(d), wspec(nw)] + [wspec(w) for w in wts] + [tab_spec] * 6,
        out_specs=[row_spec(w) for w in out_widths],
        out_shape=[jax.ShapeDtypeStruct((nb, r, w), BF16) for w in out_widths],
        compiler_params=pltpu.CompilerParams(
            dimension_semantics=("arbitrary", "arbitrary"), vmem_limit_bytes=VMEM_LIMIT),
        name="in_proj",
    )(rows, nw, *wts, *tabs)


def _retention_kernel(lgf_ref, lgb_ref, q_ref, k_ref, v_ref, sg_ref, km_ref, vm_ref, gnw_ref,
                      o_ref, sb_all, sf_run, sb_run, dmask, wqf, wqb, wkf, wkb, gcf, gcb,
                      *, cpb):
    p = pl.program_id(1)
    n = pl.program_id(2)
    nblk = pl.num_programs(2)
    c_len = CHUNK
    row = lax.broadcasted_iota(jnp.int32, (c_len, LANES), 0).astype(F32)
    col = lax.broadcasted_iota(jnp.int32, (c_len, LANES), 1).astype(F32)
    lane_i = lax.broadcasted_iota(jnp.int32, (c_len, LANES), 1)
    tn = (((0,), (0,)), ((), ()))

    def head_lanes(h):
        lo = (h % 2) * RET_DK
        return (lane_i >= lo) & (lane_i < lo + RET_DK)

    @pl.when((p == 0) & (n == 0))
    def _tables():
        rel = row - col
        for h in range(RET_HEADS):
            lf = lgf_ref[h]
            lb = lgb_ref[h]
            dmask[h] = jnp.exp(jnp.where(rel >= 0, lf * rel, -lb * rel))
            hm = head_lanes(h)
            wqf[h] = jnp.where(hm, jnp.exp(lf * (row + 1.0)), 0.0)
            wqb[h] = jnp.where(hm, jnp.exp(lb * (c_len - row)), 0.0)
            wkf[h] = jnp.exp(lf * (c_len - 1.0 - row))
            wkb[h] = jnp.exp(lb * row)
            gcf[h] = jnp.exp(jnp.full((8, LANES), lf * c_len, F32))
            gcb[h] = jnp.exp(jnp.full((8, LANES), lb * c_len, F32))
            sb_run[h] = jnp.zeros((c_len, RET_DV), F32)

    def k_pair(c, h):
        return k_ref[0, c * c_len:(c + 1) * c_len, (h // 2) * LANES:(h // 2 + 1) * LANES]

    def v_head(c, h):
        return v_ref[0, c * c_len:(c + 1) * c_len, h * RET_DV:(h + 1) * RET_DV]

    @pl.when(p == 0)
    def _backward_states():
        blk = nblk - 1 - n
        for c in range(cpb - 1, -1, -1):
            g = blk * cpb + c
            for h in range(RET_HEADS):
                sb_all[g, h] = sb_run[h].astype(BF16)
                kw = (k_pair(c, h).astype(F32) * wkb[h]).astype(BF16)
                kv = lax.dot_general(kw, v_head(c, h), tn, preferred_element_type=F32)
                sb_run[h] = gcb[h][0:1, :] * sb_run[h] + kv

    @pl.when(p == 1)
    def _forward_outputs():
        @pl.when(n == 0)
        def _meta_state():
            mrow = lax.broadcasted_iota(jnp.int32, (N_META, LANES), 0).astype(F32)
            for h in range(RET_HEADS):
                wm = jnp.exp(lgf_ref[h] * (N_META - 1.0 - mrow))
                kmw = (km_ref[:, (h // 2) * LANES:(h // 2 + 1) * LANES].astype(F32) * wm).astype(BF16)
                vmh = vm_ref[:, h * RET_DV:(h + 1) * RET_DV]
                sf_run[h] = lax.dot_general(kmw, vmh, tn, preferred_element_type=F32)

        for c in range(cpb):
            g = n * cpb + c
            rows = slice(c * c_len, (c + 1) * c_len)
            for h in range(RET_HEADS):
                qp = q_ref[0, rows, (h // 2) * LANES:(h // 2 + 1) * LANES].astype(F32)
                kp = k_pair(c, h)
                vh = v_head(c, h)
                qh = jnp.where(head_lanes(h), qp, 0.0).astype(BF16)
                s = lax.dot_general(qh, kp, (((1,), (1,)), ((), ())), preferred_element_type=F32)
                o = jnp.dot((s * dmask[h]).astype(BF16), vh, preferred_element_type=F32)
                o = o + jnp.dot((qp * wqf[h]).astype(BF16), sf_run[h].astype(BF16),
                                preferred_element_type=F32)
                o = o + jnp.dot((qp * wqb[h]).astype(BF16), sb_all[g, h],
                                preferred_element_type=F32)
                kw = (kp.astype(F32) * wkf[h]).astype(BF16)
                kv = lax.dot_general(kw, vh, tn, preferred_element_type=F32)
                sf_run[h] = gcf[h][0:1, :] * sf_run[h] + kv
                mu = jnp.mean(o, axis=-1, keepdims=True)
                d = o - mu
                var = jnp.mean(d * d, axis=-1, keepdims=True)
                yn = d * lax.rsqrt(var + GN_EPS)
                hs = slice(h * RET_DV, (h + 1) * RET_DV)
                y = yn * gnw_ref[:, hs] * sg_ref[0, rows, hs].astype(F32)
                o_ref[0, rows, hs] = y.astype(BF16)


def _retention(lgf, lgb, rq, rk, rv, sg, km, vm, gnw, cpb):
    b, s, qk_w = rq.shape
    v_w = rv.shape[2]
    rb = cpb * CHUNK
    nblk = s // rb
    n_chunks = s // CHUNK
    q_idx = lambda bi, p, n, *_: (bi, jnp.where(p == 0, 0, n), 0)
    kv_idx = lambda bi, p, n, *_: (bi, jnp.where(p == 0, nblk - 1 - n, n), 0)
    const = lambda bi, p, n, *_: (0, 0)
    tab = lambda rows_: pltpu.VMEM((RET_HEADS, rows_, LANES), F32)
    grid_spec = pltpu.PrefetchScalarGridSpec(
        num_scalar_prefetch=2,
        grid=(b, 2, nblk),
        in_specs=[
            pl.BlockSpec((1, rb, qk_w), q_idx),
            pl.BlockSpec((1, rb, qk_w), kv_idx),
            pl.BlockSpec((1, rb, v_w), kv_idx),
            pl.BlockSpec((1, rb, v_w), q_idx),
            pl.BlockSpec(km.shape, const),
            pl.BlockSpec(vm.shape, const),
            pl.BlockSpec(gnw.shape, const),
        ],
        out_specs=pl.BlockSpec((1, rb, v_w), q_idx),
        scratch_shapes=[
            pltpu.VMEM((n_chunks, RET_HEADS, LANES, RET_DV), BF16),
            tab(LANES), tab(LANES),
            tab(CHUNK), tab(CHUNK), tab(CHUNK), tab(CHUNK), tab(CHUNK),
            tab(8), tab(8),
        ],
    )
    return pl.pallas_call(
        functools.partial(_retention_kernel, cpb=cpb),
        grid_spec=grid_spec,
        out_shape=jax.ShapeDtypeStruct((b, s, v_w), BF16),
        compiler_params=pltpu.CompilerParams(
            dimension_semantics=("arbitrary", "arbitrary", "arbitrary"),
            vmem_limit_bytes=VMEM_LIMIT),
        name="retention",
    )(lgf, lgb, rq, rk, rv, sg, km, vm, gnw)


def _attn_kernel(q_ref, k_ref, v_ref, km_ref, vm_ref, o_ref, *, tk):
    nt = (((1,), (1,)), ((), ()))
    q = q_ref[0]

    s0 = lax.dot_general(q, km_ref[...], nt, preferred_element_type=F32)
    m0 = jnp.max(s0, axis=1, keepdims=True)
    acc0 = jnp.dot(jnp.exp2(s0 - m0).astype(BF16), vm_ref[...], preferred_element_type=F32)

    m, acc = m0, acc0
    for j in range(k_ref.shape[1] // tk):
        kt = k_ref[0, j * tk:(j + 1) * tk, :]
        vt = v_ref[0, j * tk:(j + 1) * tk, :]
        s = lax.dot_general(q, kt, nt, preferred_element_type=F32)
        m_new = jnp.maximum(m, jnp.max(s, axis=1, keepdims=True))
        p = jnp.exp2(s - m_new).astype(BF16)
        acc = jnp.exp2(m - m_new) * acc + jnp.dot(p, vt, preferred_element_type=F32)
        m = m_new
    o_ref[0] = (acc / acc[:, ONES_LANE:ONES_LANE + 1]).astype(BF16)


def _attention(q, k, v, km, vm, tq, tk):
    b, s, w = q.shape
    heads = w // LANES
    return pl.pallas_call(
        functools.partial(_attn_kernel, tk=tk),
        grid=(b, heads, s // tq),
        in_specs=[
            pl.BlockSpec((1, tq, LANES), lambda bi, h, i: (bi, i, h)),
            pl.BlockSpec((1, s, LANES), lambda bi, h, i: (bi, 0, h)),
            pl.BlockSpec((1, s, LANES), lambda bi, h, i: (bi, 0, h)),
            pl.BlockSpec((N_META, LANES), lambda bi, h, i: (0, h)),
            pl.BlockSpec((N_META, LANES), lambda bi, h, i: (0, h)),
        ],
        out_specs=pl.BlockSpec((1, tq, LANES), lambda bi, h, i: (bi, i, h)),
        out_shape=jax.ShapeDtypeStruct((b, s, w), BF16),
        compiler_params=pltpu.CompilerParams(
            dimension_semantics=("arbitrary", "arbitrary", "arbitrary"),
            vmem_limit_bytes=VMEM_LIMIT),
        name="mla_attention",
    )(q, k, v, km, vm)


def _merge_ffn_kernel(x_ref, yr_ref, ya_ref, gr_ref, gm_ref, w_ro, w_mo, w_o, nfw_ref,
                      w_g, w_u, w_d, nfin_ref, o_ref, *, ffn_chunks):
    def mm(a, w):
        return jnp.dot(a, w[...], preferred_element_type=F32)

    lane = lax.broadcasted_iota(jnp.int32, (ya_ref.shape[1], LANES), 1)
    packed = []
    for i in range(MLA_HEADS // 2):
        even = ya_ref[0, :, (2 * i) * LANES:(2 * i + 1) * LANES].astype(F32)
        odd = ya_ref[0, :, (2 * i + 1) * LANES:(2 * i + 2) * LANES].astype(F32)
        packed.append(jnp.where(lane < MLA_DV, even, pltpu.roll(odd, MLA_DV, axis=1)))
    ya = jnp.concatenate(packed, axis=1).astype(BF16)
    merged = (gr_ref[0].astype(F32) * mm(yr_ref[0], w_ro)
              + gm_ref[0].astype(F32) * mm(ya, w_mo))
    h1 = x_ref[0] + mm(merged.astype(BF16), w_o)
    u = _rms(h1, nfw_ref[...]).astype(BF16)
    h2 = h1
    n_tiles = w_g.shape[1] // MXU_TILE
    edges = [MXU_TILE * ((n_tiles * c + ffn_chunks - 1) // ffn_chunks) for c in range(ffn_chunks + 1)]
    for lo, hi in zip(edges[:-1], edges[1:]):
        cols = slice(lo, hi)
        g = jnp.dot(u, w_g[:, cols], preferred_element_type=F32)
        up = jnp.dot(u, w_u[:, cols], preferred_element_type=F32)
        act = (g * _sigmoid(g) * up).astype(BF16)
        h2 = h2 + jnp.dot(act, w_d[cols, :], preferred_element_type=F32)
    o_ref[0] = _rms(h2, nfin_ref[...])


def _merge_ffn(x, yr, ya, gr, gm, wts, tm, ffn_chunks):
    b, s, d = x.shape
    const = lambda bi, i: (0, 0)
    wspec = lambda w: pl.BlockSpec(w.shape, const, pipeline_mode=pl.Buffered(1))
    row = pl.BlockSpec((1, tm, d), lambda bi, i: (bi, i, 0))
    return pl.pallas_call(
        functools.partial(_merge_ffn_kernel, ffn_chunks=ffn_chunks),
        grid=(b, s // tm),
        in_specs=[row] * 5 + [wspec(w) for w in wts],
        out_specs=row,
        out_shape=jax.ShapeDtypeStruct((b, s, d), F32),
        compiler_params=pltpu.CompilerParams(
            dimension_semantics=("arbitrary", "arbitrary"), vmem_limit_bytes=VMEM_LIMIT),
        name="merge_ffn",
    )(x, yr, ya, gr, gm, *wts)


def _rope_tables(n_pos):
    pos = np.arange(n_pos, dtype=np.float64)[:, None]
    lane = np.arange(LANES)

    def angles(half):
        inv = ROPE_BASE ** (-np.arange(half, dtype=np.float64) / half)
        return pos * inv[None, :]

    half = RET_DK // 2
    ang = angles(half)[:, lane % half]
    first = (lane % RET_DK) < half
    r_cos = np.cos(ang)
    r_lo = np.where(first, -np.sin(ang), 0.0)
    r_hi = np.where(first, 0.0, np.sin(ang))
    half = MLA_ROPE // 2
    ang = angles(half)[:, lane % half]
    in_rope = (lane >= MLA_NOPE) & (lane < MLA_QK)
    first = in_rope & (lane < MLA_NOPE + half)
    second = in_rope & (lane >= MLA_NOPE + half)
    m_cos = np.where(in_rope, np.cos(ang), 1.0)
    m_lo = np.where(first, -np.sin(ang), 0.0)
    m_hi = np.where(second, np.sin(ang), 0.0)
    return tuple(t.astype(np.float32) for t in (r_cos, r_lo, r_hi, m_cos, m_lo, m_hi))


def _pad_heads(w, heads, width):
    k = w.shape[0]
    w = w.reshape(k, heads, width)
    return jnp.pad(w, ((0, 0), (0, 0), (0, LANES - width))).reshape(k, heads * LANES)


def kernel(x, meta_tokens, norm_mix_w, w_in, ret_decay_fwd, ret_decay_bwd, ret_gn_w, w_ret_out, mla_q_norm_w, w_uq, mla_kv_norm_w, w_uk, w_uv, w_mla_out, w_o, norm_ffn_w, w_ffn_gate, w_ffn_up, w_ffn_down, norm_final_w):
    b, s, d = x.shape
    assert w_in.shape[0] == 1, "one layer: the meta rows are dropped after it"
    ret_qk_w = RET_HEADS * RET_DK
    ret_v_w = RET_HEADS * RET_DV
    q_rank = w_uq.shape[1]
    kv_rank = w_uk.shape[1]
    sizes = (ret_qk_w, ret_qk_w, ret_v_w, ret_v_w, q_rank, kv_rank, MLA_ROPE, d, d)
    assert sum(sizes) == w_in.shape[2]
    bounds = [0]
    for sz in sizes:
        bounds.append(bounds[-1] + sz)
    w_rq, w_rk, w_rv, w_rg, w_cq, w_ckv, w_kr, w_gr, w_gm = (
        w_in[0][:, lo:hi].astype(BF16) for lo, hi in zip(bounds[:-1], bounds[1:]))
    w_lat = jnp.concatenate(
        [w_cq, w_ckv, jnp.pad(w_kr, ((0, 0), (MLA_NOPE, LANES - MLA_QK)))], axis=1)
    row2 = lambda a: a.reshape(1, -1).astype(F32)
    wts1 = (w_rq, w_rk, w_rv, w_rg, w_lat, w_gr, w_gm,
            row2(mla_q_norm_w[0]), _pad_heads(w_uq[0], MLA_HEADS, MLA_QK).astype(BF16),
            row2(mla_kv_norm_w[0]), _pad_heads(w_uk[0], MLA_HEADS, MLA_NOPE).astype(BF16),
            _pad_heads(w_uv[0], MLA_HEADS, MLA_DV).astype(BF16))
    nw = row2(norm_mix_w[0])
    q_scale = (MLA_QK ** -0.5) * math.log2(math.e)

    tabs = _rope_tables(s + N_META)
    tabs_meta = tuple(t[:N_META] for t in tabs)
    tabs_x = tuple(t[N_META:] for t in tabs)
    rq, rk, rv, sg, q, k, v, gr, gm = _in_proj(x, tabs_x, nw, wts1, q_scale, tm=512)
    meta = meta_tokens.astype(x.dtype)[None]
    _, rk_m, rv_m, _, _, k_m, v_m, _, _ = _in_proj(meta, tabs_meta, nw, wts1, q_scale, tm=N_META)

    lgf = -jnp.exp(ret_decay_fwd[0].astype(F32))
    lgb = -jnp.exp(ret_decay_bwd[0].astype(F32))
    y_ret = _retention(lgf, lgb, rq, rk, rv, sg, rk_m[0], rv_m[0], row2(ret_gn_w[0]), cpb=4)

    y_att = _attention(q, k, v, k_m[0], v_m[0], tq=1024, tk=2048)

    wts4 = (w_ret_out[0].astype(BF16), w_mla_out[0].astype(BF16), w_o[0].astype(BF16),
            row2(norm_ffn_w[0]), w_ffn_gate[0].astype(BF16), w_ffn_up[0].astype(BF16),
            w_ffn_down[0].astype(BF16), row2(norm_final_w))
    return _merge_ffn(x, y_ret, y_att, gr, gm, wts4, tm=512, ffn_chunks=2)
```

```python
import functools
import math

import jax
import jax.numpy as jnp
import numpy as np
from jax import lax
from jax.experimental import pallas as pl
from jax.experimental.pallas import tpu as pltpu

N_META = 16
CHUNK = 128
RET_HEADS = 8
RET_DK = 64
RET_DV = 128
MLA_HEADS = 8
MLA_NOPE = 64
MLA_ROPE = 32
MLA_DV = 64
MLA_QK = MLA_NOPE + MLA_ROPE
ROPE_BASE = 10000.0
RMS_EPS = 1e-6
GN_EPS = 1e-5

LANES = 128
MXU_TILE = 256
ONES_LANE = MLA_DV
VMEM_LIMIT = 56 * 1024 * 1024

F32 = jnp.float32
BF16 = jnp.bfloat16


def _rms(x, w):
    return x * lax.rsqrt(jnp.mean(x * x, axis=-1, keepdims=True) + RMS_EPS) * w


def _sigmoid(x):
    return 1.0 / (1.0 + jnp.exp(-x))


def _rope_group(x, cos, sin_lo, sin_hi, half):
    return (x * cos + pltpu.roll(x, LANES - half, axis=1) * sin_lo
            + pltpu.roll(x, half, axis=1) * sin_hi)


def _in_proj_kernel(x_ref, nw_ref, w_rq, w_rk, w_rv, w_rg, w_lat, w_gr, w_gm,
                    qnw_ref, w_uq, kvnw_ref, w_uk, w_uv,
                    rc_ref, rlo_ref, rhi_ref, mc_ref, mlo_ref, mhi_ref,
                    rq_o, rk_o, rv_o, sg_o, q_o, k_o, v_o, gr_o, gm_o, *, q_scale):
    u = _rms(x_ref[0], nw_ref[...]).astype(BF16)

    def mm(a, w):
        return jnp.dot(a, w[...], preferred_element_type=F32)

    rc, rlo, rhi = rc_ref[...], rlo_ref[...], rhi_ref[...]
    mc, mlo, mhi = mc_ref[...], mlo_ref[...], mhi_ref[...]

    lat = mm(u, w_lat)
    q_rank = qnw_ref.shape[1]
    kv_rank = kvnw_ref.shape[1]
    cq = _rms(lat[:, :q_rank], qnw_ref[...]).astype(BF16)
    ckv = _rms(lat[:, q_rank:q_rank + kv_rank], kvnw_ref[...]).astype(BF16)
    kr = _rope_group(lat[:, q_rank + kv_rank:], mc, mlo, mhi, MLA_ROPE // 2)

    rg = mm(u, w_rg)
    sg_o[0] = (rg * _sigmoid(rg)).astype(BF16)
    gr_o[0] = _sigmoid(mm(u, w_gr)).astype(BF16)
    gm_o[0] = _sigmoid(mm(u, w_gm)).astype(BF16)

    rq = mm(u, w_rq)
    rk = mm(u, w_rk)
    for c in range(rq.shape[1] // LANES):
        sl = slice(c * LANES, (c + 1) * LANES)
        rq_o[0, :, sl] = _rope_group(rq[:, sl], rc, rlo, rhi, RET_DK // 2).astype(BF16)
        rk_o[0, :, sl] = (_rope_group(rk[:, sl], rc, rlo, rhi, RET_DK // 2)
                          * (RET_DK ** -0.5)).astype(BF16)

    q = mm(cq, w_uq)
    kn = mm(ckv, w_uk)
    for h in range(MLA_HEADS):
        sl = slice(h * LANES, (h + 1) * LANES)
        q_o[0, :, sl] = (_rope_group(q[:, sl], mc, mlo, mhi, MLA_ROPE // 2) * q_scale).astype(BF16)
        k_o[0, :, sl] = (kn[:, sl] + kr).astype(BF16)
    v = mm(ckv, w_uv)
    lane = lax.broadcasted_iota(jnp.int32, v.shape, 1)
    v_o[0] = jnp.where(lane % LANES == ONES_LANE, 1.0, v).astype(BF16)

    rv_o[0] = mm(u, w_rv).astype(BF16)


def _in_proj(rows, tabs, nw, wts, q_scale, tm):
    nb, r, d = rows.shape
    grid = (r // tm, nb)
    const = lambda i, b: (0, 0)

    def wspec(w):
        return pl.BlockSpec(w.shape, const, pipeline_mode=pl.Buffered(1))

    row_spec = lambda width: pl.BlockSpec((1, tm, width), lambda i, b: (b, i, 0))
    tab_spec = pl.BlockSpec((tm, LANES), lambda i, b: (i, 0))
    (w_rq, w_rk, w_rv, w_rg, w_lat, w_gr, w_gm, qnw, w_uq, kvnw, w_uk, w_uv) = wts
    out_widths = (w_rq.shape[1], w_rk.shape[1], w_rv.shape[1], w_rg.shape[1],
                  w_uq.shape[1], w_uk.shape[1], w_uv.shape[1], w_gr.shape[1], w_gm.shape[1])
    return pl.pallas_call(
        functools.partial(_in_proj_kernel, q_scale=q_scale),
        grid=grid,
        in_specs=[row_spec(d), wspec(nw)] + [wspec(w) for w in wts] + [tab_spec] * 6,
        out_specs=[row_spec(w) for w in out_widths],
        out_shape=[jax.ShapeDtypeStruct((nb, r, w), BF16) for w in out_widths],
        compiler_params=pltpu.CompilerParams(
            dimension_semantics=("arbitrary", "arbitrary"), vmem_limit_bytes=VMEM_LIMIT),
        name="in_proj",
    )(rows, nw, *wts, *tabs)


def _retention_kernel(lgf_ref, lgb_ref, q_ref, k_ref, v_ref, sg_ref, km_ref, vm_ref, gnw_ref,
                      o_ref, sb_all, sf_run, sb_run, dmask, wqf, wqb, wkf, wkb, gcf, gcb,
                      *, cpb):
    p = pl.program_id(1)
    n = pl.program_id(2)
    nblk = pl.num_programs(2)
    c_len = CHUNK
    n_pairs = RET_HEADS // 2
    pair_w = 2 * RET_DV
    tn = (((0,), (0,)), ((), ()))
    nt = (((1,), (1,)), ((), ()))
    row = lax.broadcasted_iota(jnp.int32, (c_len, LANES), 0).astype(F32)
    lane_i = lax.broadcasted_iota(jnp.int32, (c_len, LANES), 1)
    odd_lane = lane_i >= RET_DK
    row2 = lax.broadcasted_iota(jnp.int32, (c_len, pair_w), 0)
    col2 = lax.broadcasted_iota(jnp.int32, (c_len, pair_w), 1)
    odd_col = col2 >= RET_DV
    own_head = (row2 >= RET_DK) == odd_col

    @pl.when((p == 0) & (n == 0))
    def _tables():
        rel = (row2 - jnp.where(odd_col, col2 - RET_DV, col2)).astype(F32)
        for i in range(n_pairs):
            lf = jnp.where(odd_lane, lgf_ref[2 * i + 1], lgf_ref[2 * i])
            lb = jnp.where(odd_lane, lgb_ref[2 * i + 1], lgb_ref[2 * i])
            lf2 = jnp.where(odd_col, lgf_ref[2 * i + 1], lgf_ref[2 * i])
            lb2 = jnp.where(odd_col, lgb_ref[2 * i + 1], lgb_ref[2 * i])
            dmask[i] = jnp.exp(jnp.where(rel >= 0, lf2 * rel, -lb2 * rel))
            wqf[i] = jnp.exp(lf * (row + 1.0))
            wqb[i] = jnp.exp(lb * (c_len - row))
            wkf[i] = jnp.exp(lf * (c_len - 1.0 - row))
            wkb[i] = jnp.exp(lb * row)
            odd8 = lax.broadcasted_iota(jnp.int32, (8, pair_w), 1) >= RET_DV
            gcf[i] = jnp.exp(jnp.where(odd8, lgf_ref[2 * i + 1], lgf_ref[2 * i]) * c_len)
            gcb[i] = jnp.exp(jnp.where(odd8, lgb_ref[2 * i + 1], lgb_ref[2 * i]) * c_len)
            sb_run[i] = jnp.zeros((c_len, pair_w), F32)

    def pair(ref, c, i, width):
        return ref[0, c * c_len:(c + 1) * c_len, i * width:(i + 1) * width]

    def state_increment(kp, w, vp):
        kw = (kp.astype(F32) * w).astype(BF16)
        kv = lax.dot_general(kw, vp, tn, preferred_element_type=F32)
        return jnp.where(own_head, kv, 0.0)

    @pl.when(p == 0)
    def _backward_states():
        blk = nblk - 1 - n
        for i in range(n_pairs):
            state = sb_run[i]
            for c in range(cpb - 1, -1, -1):
                sb_all[blk * cpb + c, i] = state.astype(BF16)
                inc = state_increment(pair(k_ref, c, i, LANES), wkb[i], pair(v_ref, c, i, pair_w))
                state = gcb[i][0:1, :] * state + inc
            sb_run[i] = state

    @pl.when(p == 1)
    def _forward_outputs():
        @pl.when(n == 0)
        def _meta_state():
            mrow = lax.broadcasted_iota(jnp.int32, (N_META, LANES), 0).astype(F32)
            odd_m = lax.broadcasted_iota(jnp.int32, (N_META, LANES), 1) >= RET_DK
            for i in range(n_pairs):
                lf = jnp.where(odd_m, lgf_ref[2 * i + 1], lgf_ref[2 * i])
                wm = jnp.exp(lf * (N_META - 1.0 - mrow))
                kmw = (km_ref[:, i * LANES:(i + 1) * LANES].astype(F32) * wm).astype(BF16)
                kv = lax.dot_general(kmw, vm_ref[:, i * pair_w:(i + 1) * pair_w], tn,
                                     preferred_element_type=F32)
                sf_run[i] = jnp.where(own_head, kv, 0.0)

        zeros = jnp.zeros((c_len, RET_DV), BF16)
        for i in range(n_pairs):
            state = sf_run[i]
            for c in range(cpb):
                g = n * cpb + c
                rows = slice(c * c_len, (c + 1) * c_len)
                qb16 = pair(q_ref, c, i, LANES)
                kp = pair(k_ref, c, i, LANES)
                vp = pair(v_ref, c, i, pair_w)
                qp = qb16.astype(F32)
                kpf = kp.astype(F32)
                k2 = jnp.concatenate([jnp.where(odd_lane, 0.0, kpf).astype(BF16),
                                      jnp.where(odd_lane, kpf, 0.0).astype(BF16)], axis=0)
                s2 = lax.dot_general(qb16, k2, nt, preferred_element_type=F32)
                v_diag = jnp.concatenate([
                    jnp.concatenate([vp[:, :RET_DV], zeros], axis=1),
                    jnp.concatenate([zeros, vp[:, RET_DV:]], axis=1)], axis=0)
                q_cross = jnp.concatenate([(qp * wqf[i]).astype(BF16),
                                           (qp * wqb[i]).astype(BF16)], axis=1)
                states = jnp.concatenate([state.astype(BF16), sb_all[g, i]], axis=0)
                o2 = (jnp.dot((s2 * dmask[i]).astype(BF16), v_diag, preferred_element_type=F32)
                      + jnp.dot(q_cross, states, preferred_element_type=F32))
                state = gcf[i][0:1, :] * state + state_increment(kp, wkf[i], vp)
                for j in range(2):
                    o = o2[:, j * RET_DV:(j + 1) * RET_DV]
                    mu = jnp.mean(o, axis=-1, keepdims=True)
                    d = o - mu
                    var = jnp.mean(d * d, axis=-1, keepdims=True)
                    yn = d * lax.rsqrt(var + GN_EPS)
                    hs = slice((2 * i + j) * RET_DV, (2 * i + j + 1) * RET_DV)
                    y = yn * gnw_ref[:, hs] * sg_ref[0, rows, hs].astype(F32)
                    o_ref[0, rows, hs] = y.astype(BF16)
            sf_run[i] = state


def _retention(lgf, lgb, rq, rk, rv, sg, km, vm, gnw, cpb):
    b, s, qk_w = rq.shape
    v_w = rv.shape[2]
    rb = cpb * CHUNK
    nblk = s // rb
    n_chunks = s // CHUNK
    q_idx = lambda bi, p, n, *_: (bi, jnp.where(p == 0, 0, n), 0)
    kv_idx = lambda bi, p, n, *_: (bi, jnp.where(p == 0, nblk - 1 - n, n), 0)
    const = lambda bi, p, n, *_: (0, 0)
    n_pairs = RET_HEADS // 2
    pair_w = 2 * RET_DV
    tab = lambda rows_, width: pltpu.VMEM((n_pairs, rows_, width), F32)
    grid_spec = pltpu.PrefetchScalarGridSpec(
        num_scalar_prefetch=2,
        grid=(b, 2, nblk),
        in_specs=[
            pl.BlockSpec((1, rb, qk_w), q_idx),
            pl.BlockSpec((1, rb, qk_w), kv_idx),
            pl.BlockSpec((1, rb, v_w), kv_idx),
            pl.BlockSpec((1, rb, v_w), q_idx),
            pl.BlockSpec(km.shape, const),
            pl.BlockSpec(vm.shape, const),
            pl.BlockSpec(gnw.shape, const),
        ],
        out_specs=pl.BlockSpec((1, rb, v_w), q_idx),
        scratch_shapes=[
            pltpu.VMEM((n_chunks, n_pairs, LANES, pair_w), BF16),
            tab(LANES, pair_w), tab(LANES, pair_w),
            tab(CHUNK, pair_w),
            tab(CHUNK, LANES), tab(CHUNK, LANES), tab(CHUNK, LANES), tab(CHUNK, LANES),
            tab(8, pair_w), tab(8, pair_w),
        ],
    )
    return pl.pallas_call(
        functools.partial(_retention_kernel, cpb=cpb),
        grid_spec=grid_spec,
        out_shape=jax.ShapeDtypeStruct((b, s, v_w), BF16),
        compiler_params=pltpu.CompilerParams(
            dimension_semantics=("arbitrary", "arbitrary", "arbitrary"),
            vmem_limit_bytes=VMEM_LIMIT),
        name="retention",
    )(lgf, lgb, rq, rk, rv, sg, km, vm, gnw)


def _attn_kernel(q_ref, k_ref, v_ref, km_ref, vm_ref, o_ref, *, tk):
    nt = (((1,), (1,)), ((), ()))
    q = q_ref[0]

    s0 = lax.dot_general(q, km_ref[...], nt, preferred_element_type=F32)
    m0 = jnp.max(s0, axis=1, keepdims=True)
    acc0 = jnp.dot(jnp.exp2(s0 - m0).astype(BF16), vm_ref[...], preferred_element_type=F32)

    m, acc = m0, acc0
    for j in range(k_ref.shape[1] // tk):
        kt = k_ref[0, j * tk:(j + 1) * tk, :]
        vt = v_ref[0, j * tk:(j + 1) * tk, :]
        s = lax.dot_general(q, kt, nt, preferred_element_type=F32)
        m_new = jnp.maximum(m, jnp.max(s, axis=1, keepdims=True))
        p = jnp.exp2(s - m_new).astype(BF16)
        acc = jnp.exp2(m - m_new) * acc + jnp.dot(p, vt, preferred_element_type=F32)
        m = m_new
    o_ref[0] = (acc / acc[:, ONES_LANE:ONES_LANE + 1]).astype(BF16)


def _attention(q, k, v, km, vm, tq, tk):
    b, s, w = q.shape
    heads = w // LANES
    return pl.pallas_call(
        functools.partial(_attn_kernel, tk=tk),
        grid=(b, heads, s // tq),
        in_specs=[
            pl.BlockSpec((1, tq, LANES), lambda bi, h, i: (bi, i, h)),
            pl.BlockSpec((1, s, LANES), lambda bi, h, i: (bi, 0, h)),
            pl.BlockSpec((1, s, LANES), lambda bi, h, i: (bi, 0, h)),
            pl.BlockSpec((N_META, LANES), lambda bi, h, i: (0, h)),
            pl.BlockSpec((N_META, LANES), lambda bi, h, i: (0, h)),
        ],
        out_specs=pl.BlockSpec((1, tq, LANES), lambda bi, h, i: (bi, i, h)),
        out_shape=jax.ShapeDtypeStruct((b, s, w), BF16),
        compiler_params=pltpu.CompilerParams(
            dimension_semantics=("arbitrary", "arbitrary", "arbitrary"),
            vmem_limit_bytes=VMEM_LIMIT),
        name="mla_attention",
    )(q, k, v, km, vm)


def _merge_ffn_kernel(x_ref, yr_ref, ya_ref, gr_ref, gm_ref, w_ro, w_mo, w_o, nfw_ref,
                      w_g, w_u, w_d, nfin_ref, o_ref, *, ffn_chunks):
    def mm(a, w):
        return jnp.dot(a, w[...], preferred_element_type=F32)

    lane = lax.broadcasted_iota(jnp.int32, (ya_ref.shape[1], LANES), 1)
    packed = []
    for i in range(MLA_HEADS // 2):
        even = ya_ref[0, :, (2 * i) * LANES:(2 * i + 1) * LANES].astype(F32)
        odd = ya_ref[0, :, (2 * i + 1) * LANES:(2 * i + 2) * LANES].astype(F32)
        packed.append(jnp.where(lane < MLA_DV, even, pltpu.roll(odd, MLA_DV, axis=1)))
    ya = jnp.concatenate(packed, axis=1).astype(BF16)
    merged = (gr_ref[0].astype(F32) * mm(yr_ref[0], w_ro)
              + gm_ref[0].astype(F32) * mm(ya, w_mo))
    h1 = x_ref[0] + mm(merged.astype(BF16), w_o)
    u = _rms(h1, nfw_ref[...]).astype(BF16)
    h2 = h1
    n_tiles = w_g.shape[1] // MXU_TILE
    edges = [MXU_TILE * ((n_tiles * c + ffn_chunks - 1) // ffn_chunks) for c in range(ffn_chunks + 1)]
    for lo, hi in zip(edges[:-1], edges[1:]):
        cols = slice(lo, hi)
        g = jnp.dot(u, w_g[:, cols], preferred_element_type=F32)
        up = jnp.dot(u, w_u[:, cols], preferred_element_type=F32)
        act = (g * _sigmoid(g) * up).astype(BF16)
        h2 = h2 + jnp.dot(act, w_d[cols, :], preferred_element_type=F32)
    o_ref[0] = _rms(h2, nfin_ref[...])


def _merge_ffn(x, yr, ya, gr, gm, wts, tm, ffn_chunks):
    b, s, d = x.shape
    const = lambda bi, i: (0, 0)
    wspec = lambda w: pl.BlockSpec(w.shape, const, pipeline_mode=pl.Buffered(1))
    row = pl.BlockSpec((1, tm, d), lambda bi, i: (bi, i, 0))
    return pl.pallas_call(
        functools.partial(_merge_ffn_kernel, ffn_chunks=ffn_chunks),
        grid=(b, s // tm),
        in_specs=[row] * 5 + [wspec(w) for w in wts],
        out_specs=row,
        out_shape=jax.ShapeDtypeStruct((b, s, d), F32),
        compiler_params=pltpu.CompilerParams(
            dimension_semantics=("arbitrary", "arbitrary"), vmem_limit_bytes=VMEM_LIMIT),
        name="merge_ffn",
    )(x, yr, ya, gr, gm, *wts)


def _rope_tables(n_pos):
    pos = np.arange(n_pos, dtype=np.float64)[:, None]
    lane = np.arange(LANES)

    def angles(half):
        inv = ROPE_BASE ** (-np.arange(half, dtype=np.float64) / half)
        return pos * inv[None, :]

    half = RET_DK // 2
    ang = angles(half)[:, lane % half]
    first = (lane % RET_DK) < half
    r_cos = np.cos(ang)
    r_lo = np.where(first, -np.sin(ang), 0.0)
    r_hi = np.where(first, 0.0, np.sin(ang))
    half = MLA_ROPE // 2
    ang = angles(half)[:, lane % half]
    in_rope = (lane >= MLA_NOPE) & (lane < MLA_QK)
    first = in_rope & (lane < MLA_NOPE + half)
    second = in_rope & (lane >= MLA_NOPE + half)
    m_cos = np.where(in_rope, np.cos(ang), 1.0)
    m_lo = np.where(first, -np.sin(ang), 0.0)
    m_hi = np.where(second, np.sin(ang), 0.0)
    return tuple(t.astype(np.float32) for t in (r_cos, r_lo, r_hi, m_cos, m_lo, m_hi))


def _pad_heads(w, heads, width):
    k = w.shape[0]
    w = w.reshape(k, heads, width)
    return jnp.pad(w, ((0, 0), (0, 0), (0, LANES - width))).reshape(k, heads * LANES)


def kernel(x, meta_tokens, norm_mix_w, w_in, ret_decay_fwd, ret_decay_bwd, ret_gn_w, w_ret_out, mla_q_norm_w, w_uq, mla_kv_norm_w, w_uk, w_uv, w_mla_out, w_o, norm_ffn_w, w_ffn_gate, w_ffn_up, w_ffn_down, norm_final_w):
    b, s, d = x.shape
    assert w_in.shape[0] == 1, "one layer: the meta rows are dropped after it"
    ret_qk_w = RET_HEADS * RET_DK
    ret_v_w = RET_HEADS * RET_DV
    q_rank = w_uq.shape[1]
    kv_rank = w_uk.shape[1]
    sizes = (ret_qk_w, ret_qk_w, ret_v_w, ret_v_w, q_rank, kv_rank, MLA_ROPE, d, d)
    assert sum(sizes) == w_in.shape[2]
    bounds = [0]
    for sz in sizes:
        bounds.append(bounds[-1] + sz)
    w_rq, w_rk, w_rv, w_rg, w_cq, w_ckv, w_kr, w_gr, w_gm = (
        w_in[0][:, lo:hi].astype(BF16) for lo, hi in zip(bounds[:-1], bounds[1:]))
    w_lat = jnp.concatenate(
        [w_cq, w_ckv, jnp.pad(w_kr, ((0, 0), (MLA_NOPE, LANES - MLA_QK)))], axis=1)
    row2 = lambda a: a.reshape(1, -1).astype(F32)
    wts1 = (w_rq, w_rk, w_rv, w_rg, w_lat, w_gr, w_gm,
            row2(mla_q_norm_w[0]), _pad_heads(w_uq[0], MLA_HEADS, MLA_QK).astype(BF16),
            row2(mla_kv_norm_w[0]), _pad_heads(w_uk[0], MLA_HEADS, MLA_NOPE).astype(BF16),
            _pad_heads(w_uv[0], MLA_HEADS, MLA_DV).astype(BF16))
    nw = row2(norm_mix_w[0])
    q_scale = (MLA_QK ** -0.5) * math.log2(math.e)

    tabs = _rope_tables(s + N_META)
    tabs_meta = tuple(t[:N_META] for t in tabs)
    tabs_x = tuple(t[N_META:] for t in tabs)
    rq, rk, rv, sg, q, k, v, gr, gm = _in_proj(x, tabs_x, nw, wts1, q_scale, tm=512)
    meta = meta_tokens.astype(x.dtype)[None]
    _, rk_m, rv_m, _, _, k_m, v_m, _, _ = _in_proj(meta, tabs_meta, nw, wts1, q_scale, tm=N_META)

    lgf = -jnp.exp(ret_decay_fwd[0].astype(F32))
    lgb = -jnp.exp(ret_decay_bwd[0].astype(F32))
    y_ret = _retention(lgf, lgb, rq, rk, rv, sg, rk_m[0], rv_m[0], row2(ret_gn_w[0]), cpb=4)

    y_att = _attention(q, k, v, k_m[0], v_m[0], tq=1024, tk=2048)

    wts4 = (w_ret_out[0].astype(BF16), w_mla_out[0].astype(BF16), w_o[0].astype(BF16),
            row2(norm_ffn_w[0]), w_ffn_gate[0].astype(BF16), w_ffn_up[0].astype(BF16),
            w_ffn_down[0].astype(BF16), row2(norm_final_w))
    return _merge_ffn(x, y_ret, y_att, gr, gm, wts4, tm=512, ffn_chunks=2)
```

```python
import functools
import math

import jax
import jax.numpy as jnp
import numpy as np
from jax import lax
from jax.experimental import pallas as pl
from jax.experimental.pallas import tpu as pltpu

N_META = 16
CHUNK = 128
RET_HEADS = 8
RET_DK = 64
RET_DV = 128
MLA_HEADS = 8
MLA_NOPE = 64
MLA_ROPE = 32
MLA_DV = 64
MLA_QK = MLA_NOPE + MLA_ROPE
ROPE_BASE = 10000.0
RMS_EPS = 1e-6
GN_EPS = 1e-5

LANES = 128
MXU_TILE = 256
ONES_LANE = MLA_DV
VMEM_LIMIT = 56 * 1024 * 1024

F32 = jnp.float32
BF16 = jnp.bfloat16

_TN = (((0,), (0,)), ((), ()))
_NT = (((1,), (1,)), ((), ()))


def _rms(x, w):
    return x * lax.rsqrt(jnp.mean(x * x, axis=-1, keepdims=True) + RMS_EPS) * w


def _sigmoid(x):
    return 1.0 / (1.0 + jnp.exp(-x))


def _rope_group(x, cos, sin_lo, sin_hi, half):
    return (x * cos + pltpu.roll(x, LANES - half, axis=1) * sin_lo
            + pltpu.roll(x, half, axis=1) * sin_hi)


def _in_proj_kernel(x_ref, nw_ref, w_rq, w_rk, w_rv, w_rg, w_lat, w_gr, w_gm,
                    qnw_ref, w_uq, kvnw_ref, w_uk, w_uv,
                    rc_ref, rlo_ref, rhi_ref, mc_ref, mlo_ref, mhi_ref,
                    rq_o, rk_o, rv_o, sg_o, q_o, k_o, v_o, gr_o, gm_o, *, q_scale):
    u = _rms(x_ref[0], nw_ref[...]).astype(BF16)

    def mm(a, w):
        return jnp.dot(a, w[...], preferred_element_type=F32)

    rc, rlo, rhi = rc_ref[...], rlo_ref[...], rhi_ref[...]
    mc, mlo, mhi = mc_ref[...], mlo_ref[...], mhi_ref[...]

    lat = mm(u, w_lat)
    q_rank = qnw_ref.shape[1]
    kv_rank = kvnw_ref.shape[1]
    cq = _rms(lat[:, :q_rank], qnw_ref[...]).astype(BF16)
    ckv = _rms(lat[:, q_rank:q_rank + kv_rank], kvnw_ref[...]).astype(BF16)
    kr = _rope_group(lat[:, q_rank + kv_rank:], mc, mlo, mhi, MLA_ROPE // 2)

    rg = mm(u, w_rg)
    sg_o[0] = (rg * _sigmoid(rg)).astype(BF16)
    gr_o[0] = _sigmoid(mm(u, w_gr)).astype(BF16)
    gm_o[0] = _sigmoid(mm(u, w_gm)).astype(BF16)

    rq = mm(u, w_rq)
    rk = mm(u, w_rk)
    for c in range(rq.shape[1] // LANES):
        sl = slice(c * LANES, (c + 1) * LANES)
        rq_o[0, :, sl] = _rope_group(rq[:, sl], rc, rlo, rhi, RET_DK // 2).astype(BF16)
        rk_o[0, :, sl] = (_rope_group(rk[:, sl], rc, rlo, rhi, RET_DK // 2)
                          * (RET_DK ** -0.5)).astype(BF16)

    q = mm(cq, w_uq)
    kn = mm(ckv, w_uk)
    for h in range(MLA_HEADS):
        sl = slice(h * LANES, (h + 1) * LANES)
        q_o[0, :, sl] = (_rope_group(q[:, sl], mc, mlo, mhi, MLA_ROPE // 2) * q_scale).astype(BF16)
        k_o[0, :, sl] = (kn[:, sl] + kr).astype(BF16)
    v = mm(ckv, w_uv)
    lane = lax.broadcasted_iota(jnp.int32, v.shape, 1)
    v_o[0] = jnp.where(lane % LANES == ONES_LANE, 1.0, v).astype(BF16)

    rv_o[0] = mm(u, w_rv).astype(BF16)


def _in_proj(rows, tabs, nw, wts, q_scale, tm):
    nb, r, d = rows.shape
    grid = (r // tm, nb)
    const = lambda i, b: (0, 0)

    def wspec(w):
        return pl.BlockSpec(w.shape, const, pipeline_mode=pl.Buffered(1))

    row_spec = lambda width: pl.BlockSpec((1, tm, width), lambda i, b: (b, i, 0))
    tab_spec = pl.BlockSpec((tm, LANES), lambda i, b: (i, 0))
    (w_rq, w_rk, w_rv, w_rg, w_lat, w_gr, w_gm, qnw, w_uq, kvnw, w_uk, w_uv) = wts
    out_widths = (w_rq.shape[1], w_rk.shape[1], w_rv.shape[1], w_rg.shape[1],
                  w_uq.shape[1], w_uk.shape[1], w_uv.shape[1], w_gr.shape[1], w_gm.shape[1])
    return pl.pallas_call(
        functools.partial(_in_proj_kernel, q_scale=q_scale),
        grid=grid,
        in_specs=[row_spec(d), wspec(nw)] + [wspec(w) for w in wts] + [tab_spec] * 6,
        out_specs=[row_spec(w) for w in out_widths],
        out_shape=[jax.ShapeDtypeStruct((nb, r, w), BF16) for w in out_widths],
        compiler_params=pltpu.CompilerParams(
            dimension_semantics=("arbitrary", "arbitrary"), vmem_limit_bytes=VMEM_LIMIT),
        name="in_proj",
    )(rows, nw, *wts, *tabs)


N_PAIRS = RET_HEADS // 2
PAIR_W = 2 * RET_DV


def _odd_lane():
    return lax.broadcasted_iota(jnp.int32, (CHUNK, LANES), 1) >= RET_DK


def _odd_col(rows=CHUNK):
    return lax.broadcasted_iota(jnp.int32, (rows, PAIR_W), 1) >= RET_DV


def _own_head():
    row = lax.broadcasted_iota(jnp.int32, (LANES, PAIR_W), 0)
    return (row >= RET_DK) == _odd_col(LANES)


def _chunk_pos():
    return lax.broadcasted_iota(jnp.int32, (CHUNK, LANES), 0).astype(F32)


def _pair_scalar(ref, i, odd):
    return jnp.where(odd, ref[2 * i + 1], ref[2 * i])


def _state_increment(kp, w, vp):
    kw = (kp.astype(F32) * w).astype(BF16)
    kv = lax.dot_general(kw, vp, _TN, preferred_element_type=F32)
    return jnp.where(_own_head(), kv, 0.0)


def _ret_bwd_kernel(lgb_ref, k_ref, v_ref, sb_ref, sb_run, wkb, gcb, *, cpb):
    @pl.when(pl.program_id(1) == 0)
    def _start_of_sequence():
        for i in range(N_PAIRS):
            wkb[i] = jnp.exp(_pair_scalar(lgb_ref, i, _odd_lane()) * _chunk_pos())
            gcb[i] = jnp.exp(_pair_scalar(lgb_ref, i, _odd_col(8)) * CHUNK)
            sb_run[i] = jnp.zeros((LANES, PAIR_W), F32)

    for i in range(N_PAIRS):
        state = sb_run[i]
        for c in range(cpb - 1, -1, -1):
            rows = slice(c * CHUNK, (c + 1) * CHUNK)
            sb_ref[0, c, i] = state.astype(BF16)
            inc = _state_increment(k_ref[0, rows, i * LANES:(i + 1) * LANES], wkb[i],
                                   v_ref[0, rows, i * PAIR_W:(i + 1) * PAIR_W])
            state = gcb[i][0:1, :] * state + inc
        sb_run[i] = state


def _retention_bwd_states(lgb, rk, rv, cpb):
    b, s, qk_w = rk.shape
    v_w = rv.shape[2]
    rb = cpb * CHUNK
    nblk = s // rb
    rev = lambda bi, n, *_: (bi, nblk - 1 - n, 0)
    grid_spec = pltpu.PrefetchScalarGridSpec(
        num_scalar_prefetch=1,
        grid=(b, nblk),
        in_specs=[pl.BlockSpec((1, rb, qk_w), rev), pl.BlockSpec((1, rb, v_w), rev)],
        out_specs=pl.BlockSpec((1, cpb, N_PAIRS, LANES, PAIR_W),
                               lambda bi, n, *_: (bi, nblk - 1 - n, 0, 0, 0)),
        scratch_shapes=[
            pltpu.VMEM((N_PAIRS, LANES, PAIR_W), F32),
            pltpu.VMEM((N_PAIRS, CHUNK, LANES), F32),
            pltpu.VMEM((N_PAIRS, 8, PAIR_W), F32),
        ],
    )
    return pl.pallas_call(
        functools.partial(_ret_bwd_kernel, cpb=cpb),
        grid_spec=grid_spec,
        out_shape=jax.ShapeDtypeStruct((b, s // CHUNK, N_PAIRS, LANES, PAIR_W), BF16),
        compiler_params=pltpu.CompilerParams(
            dimension_semantics=("arbitrary", "arbitrary"), vmem_limit_bytes=VMEM_LIMIT),
        name="retention_bwd_states",
    )(lgb, rk, rv)


def _retention_tables(lgf_ref, lgb_ref, dmask, wqf, wqb, wkf, gcf):
    pos = _chunk_pos()
    row2 = lax.broadcasted_iota(jnp.int32, (CHUNK, PAIR_W), 0)
    col2 = lax.broadcasted_iota(jnp.int32, (CHUNK, PAIR_W), 1)
    odd_col = _odd_col()
    rel = (row2 - jnp.where(odd_col, col2 - RET_DV, col2)).astype(F32)
    for i in range(N_PAIRS):
        lf = _pair_scalar(lgf_ref, i, _odd_lane())
        lb = _pair_scalar(lgb_ref, i, _odd_lane())
        lf2 = _pair_scalar(lgf_ref, i, odd_col)
        lb2 = _pair_scalar(lgb_ref, i, odd_col)
        dmask[i] = jnp.exp(jnp.where(rel >= 0, lf2 * rel, -lb2 * rel))
        wqf[i] = jnp.exp(lf * (pos + 1.0))
        wqb[i] = jnp.exp(lb * (CHUNK - pos))
        wkf[i] = jnp.exp(lf * (CHUNK - 1.0 - pos))
        gcf[i] = jnp.exp(_pair_scalar(lgf_ref, i, _odd_col(8)) * CHUNK)


def _retention_meta_state(lgf_ref, km_ref, vm_ref, sf_run):
    mrow = lax.broadcasted_iota(jnp.int32, (N_META, LANES), 0).astype(F32)
    odd_m = lax.broadcasted_iota(jnp.int32, (N_META, LANES), 1) >= RET_DK
    for i in range(N_PAIRS):
        wm = jnp.exp(_pair_scalar(lgf_ref, i, odd_m) * (N_META - 1.0 - mrow))
        kmw = (km_ref[:, i * LANES:(i + 1) * LANES].astype(F32) * wm).astype(BF16)
        kv = lax.dot_general(kmw, vm_ref[:, i * PAIR_W:(i + 1) * PAIR_W], _TN,
                             preferred_element_type=F32)
        sf_run[i] = jnp.where(_own_head(), kv, 0.0)


def _retention_chunk(q_ref, k_ref, v_ref, sg_ref, sb_ref, gnw_ref, o_ref,
                     sf_run, dmask, wqf, wqb, wkf, gcf):
    zeros = jnp.zeros((CHUNK, RET_DV), BF16)
    odd_lane = _odd_lane()
    for i in range(N_PAIRS):
        qb16 = q_ref[0, :, i * LANES:(i + 1) * LANES]
        kp = k_ref[0, :, i * LANES:(i + 1) * LANES]
        vp = v_ref[0, :, i * PAIR_W:(i + 1) * PAIR_W]
        qp = qb16.astype(F32)
        kpf = kp.astype(F32)
        k2 = jnp.concatenate([jnp.where(odd_lane, 0.0, kpf).astype(BF16),
                              jnp.where(odd_lane, kpf, 0.0).astype(BF16)], axis=0)
        s2 = lax.dot_general(qb16, k2, _NT, preferred_element_type=F32)
        v_diag = jnp.concatenate([
            jnp.concatenate([vp[:, :RET_DV], zeros], axis=1),
            jnp.concatenate([zeros, vp[:, RET_DV:]], axis=1)], axis=0)
        q_cross = jnp.concatenate([(qp * wqf[i]).astype(BF16),
                                   (qp * wqb[i]).astype(BF16)], axis=1)
        state = sf_run[i]
        states = jnp.concatenate([state.astype(BF16), sb_ref[0, 0, i]], axis=0)
        o2 = (jnp.dot((s2 * dmask[i]).astype(BF16), v_diag, preferred_element_type=F32)
              + jnp.dot(q_cross, states, preferred_element_type=F32))
        sf_run[i] = gcf[i][0:1, :] * state + _state_increment(kp, wkf[i], vp)
        for j in range(2):
            o = o2[:, j * RET_DV:(j + 1) * RET_DV]
            mu = jnp.mean(o, axis=-1, keepdims=True)
            d = o - mu
            var = jnp.mean(d * d, axis=-1, keepdims=True)
            yn = d * lax.rsqrt(var + GN_EPS)
            hs = slice((2 * i + j) * RET_DV, (2 * i + j + 1) * RET_DV)
            y = yn * gnw_ref[:, hs] * sg_ref[0, :, hs].astype(F32)
            o_ref[0, :, hs] = y.astype(BF16)


def _attn_ret_kernel(lgf_ref, lgb_ref, q_ref, k_ref, v_ref, km_ref, vm_ref,
                     rq_ref, rk_ref, rv_ref, sg_ref, sb_ref, rkm_ref, rvm_ref, gnw_ref,
                     o_ref, yr_ref, sf_run, dmask, wqf, wqb, wkf, gcf, *, tk):
    @pl.when((pl.program_id(1) == 0) & (pl.program_id(2) == 0))
    def _start_of_sequence():
        _retention_tables(lgf_ref, lgb_ref, dmask, wqf, wqb, wkf, gcf)
        _retention_meta_state(lgf_ref, rkm_ref, rvm_ref, sf_run)

    _retention_chunk(rq_ref, rk_ref, rv_ref, sg_ref, sb_ref, gnw_ref, yr_ref,
                     sf_run, dmask, wqf, wqb, wkf, gcf)

    q = q_ref[0]
    s0 = lax.dot_general(q, km_ref[...], _NT, preferred_element_type=F32)
    m0 = jnp.max(s0, axis=1, keepdims=True)
    acc0 = jnp.dot(jnp.exp2(s0 - m0).astype(BF16), vm_ref[...], preferred_element_type=F32)

    m, acc = m0, acc0
    for j in range(k_ref.shape[1] // tk):
        kt = k_ref[0, j * tk:(j + 1) * tk, :]
        vt = v_ref[0, j * tk:(j + 1) * tk, :]
        s = lax.dot_general(q, kt, _NT, preferred_element_type=F32)
        m_new = jnp.maximum(m, jnp.max(s, axis=1, keepdims=True))
        p = jnp.exp2(s - m_new).astype(BF16)
        acc = jnp.exp2(m - m_new) * acc + jnp.dot(p, vt, preferred_element_type=F32)
        m = m_new
    o_ref[0] = (acc / acc[:, ONES_LANE:ONES_LANE + 1]).astype(BF16)


def _attention_retention(lgf, lgb, q, k, v, km, vm, rq, rk, rv, sg, sb, rkm, rvm, gnw, tq, tk):
    b, s, w = q.shape
    heads = w // LANES
    n_q = s // tq
    assert heads * n_q == s // CHUNK, "one retention chunk per attention grid step"
    v_w = rv.shape[2]
    att = lambda bi, h, i, *_: (bi, i, h)
    kv = lambda bi, h, i, *_: (bi, 0, h)
    meta = lambda bi, h, i, *_: (0, h)
    chunk = lambda bi, h, i, *_: (bi, h * n_q + i, 0)
    const = lambda bi, h, i, *_: (0, 0)
    tab = lambda rows, width: pltpu.VMEM((N_PAIRS, rows, width), F32)
    grid_spec = pltpu.PrefetchScalarGridSpec(
        num_scalar_prefetch=2,
        grid=(b, heads, n_q),
        in_specs=[
            pl.BlockSpec((1, tq, LANES), att),
            pl.BlockSpec((1, s, LANES), kv),
            pl.BlockSpec((1, s, LANES), kv),
            pl.BlockSpec((N_META, LANES), meta),
            pl.BlockSpec((N_META, LANES), meta),
            pl.BlockSpec((1, CHUNK, rq.shape[2]), chunk),
            pl.BlockSpec((1, CHUNK, rk.shape[2]), chunk),
            pl.BlockSpec((1, CHUNK, v_w), chunk),
            pl.BlockSpec((1, CHUNK, v_w), chunk),
            pl.BlockSpec((1, 1, N_PAIRS, LANES, PAIR_W),
                         lambda bi, h, i, *_: (bi, h * n_q + i, 0, 0, 0)),
            pl.BlockSpec(rkm.shape, const),
            pl.BlockSpec(rvm.shape, const),
            pl.BlockSpec(gnw.shape, const),
        ],
        out_specs=[pl.BlockSpec((1, tq, LANES), att), pl.BlockSpec((1, CHUNK, v_w), chunk)],
        scratch_shapes=[
            tab(LANES, PAIR_W),
            tab(CHUNK, PAIR_W),
            tab(CHUNK, LANES), tab(CHUNK, LANES), tab(CHUNK, LANES),
            tab(8, PAIR_W),
        ],
    )
    return pl.pallas_call(
        functools.partial(_attn_ret_kernel, tk=tk),
        grid_spec=grid_spec,
        out_shape=[jax.ShapeDtypeStruct((b, s, w), BF16), jax.ShapeDtypeStruct((b, s, v_w), BF16)],
        compiler_params=pltpu.CompilerParams(
            dimension_semantics=("arbitrary", "arbitrary", "arbitrary"),
            vmem_limit_bytes=VMEM_LIMIT),
        name="attention_retention",
    )(lgf, lgb, q, k, v, km, vm, rq, rk, rv, sg, sb, rkm, rvm, gnw)


def _merge_ffn_kernel(x_ref, yr_ref, ya_ref, gr_ref, gm_ref, w_ro, w_mo, w_o, nfw_ref,
                      w_g, w_u, w_d, nfin_ref, o_ref, *, ffn_chunks):
    def mm(a, w):
        return jnp.dot(a, w[...], preferred_element_type=F32)

    lane = lax.broadcasted_iota(jnp.int32, (ya_ref.shape[1], LANES), 1)
    packed = []
    for i in range(MLA_HEADS // 2):
        even = ya_ref[0, :, (2 * i) * LANES:(2 * i + 1) * LANES].astype(F32)
        odd = ya_ref[0, :, (2 * i + 1) * LANES:(2 * i + 2) * LANES].astype(F32)
        packed.append(jnp.where(lane < MLA_DV, even, pltpu.roll(odd, MLA_DV, axis=1)))
    ya = jnp.concatenate(packed, axis=1).astype(BF16)
    merged = (gr_ref[0].astype(F32) * mm(yr_ref[0], w_ro)
              + gm_ref[0].astype(F32) * mm(ya, w_mo))
    h1 = x_ref[0] + mm(merged.astype(BF16), w_o)
    u = _rms(h1, nfw_ref[...]).astype(BF16)
    h2 = h1
    n_tiles = w_g.shape[1] // MXU_TILE
    edges = [MXU_TILE * ((n_tiles * c + ffn_chunks - 1) // ffn_chunks) for c in range(ffn_chunks + 1)]
    for lo, hi in zip(edges[:-1], edges[1:]):
        cols = slice(lo, hi)
        g = jnp.dot(u, w_g[:, cols], preferred_element_type=F32)
        up = jnp.dot(u, w_u[:, cols], preferred_element_type=F32)
        act = (g * _sigmoid(g) * up).astype(BF16)
        h2 = h2 + jnp.dot(act, w_d[cols, :], preferred_element_type=F32)
    o_ref[0] = _rms(h2, nfin_ref[...])


def _merge_ffn(x, yr, ya, gr, gm, wts, tm, ffn_chunks):
    b, s, d = x.shape
    const = lambda bi, i: (0, 0)
    wspec = lambda w: pl.BlockSpec(w.shape, const, pipeline_mode=pl.Buffered(1))
    row = pl.BlockSpec((1, tm, d), lambda bi, i: (bi, i, 0))
    return pl.pallas_call(
        functools.partial(_merge_ffn_kernel, ffn_chunks=ffn_chunks),
        grid=(b, s // tm),
        in_specs=[row] * 5 + [wspec(w) for w in wts],
        out_specs=row,
        out_shape=jax.ShapeDtypeStruct((b, s, d), F32),
        compiler_params=pltpu.CompilerParams(
            dimension_semantics=("arbitrary", "arbitrary"), vmem_limit_bytes=VMEM_LIMIT),
        name="merge_ffn",
    )(x, yr, ya, gr, gm, *wts)


def _rope_tables(n_pos):
    pos = np.arange(n_pos, dtype=np.float64)[:, None]
    lane = np.arange(LANES)

    def angles(half):
        inv = ROPE_BASE ** (-np.arange(half, dtype=np.float64) / half)
        return pos * inv[None, :]

    half = RET_DK // 2
    ang = angles(half)[:, lane % half]
    first = (lane % RET_DK) < half
    r_cos = np.cos(ang)
    r_lo = np.where(first, -np.sin(ang), 0.0)
    r_hi = np.where(first, 0.0, np.sin(ang))
    half = MLA_ROPE // 2
    ang = angles(half)[:, lane % half]
    in_rope = (lane >= MLA_NOPE) & (lane < MLA_QK)
    first = in_rope & (lane < MLA_NOPE + half)
    second = in_rope & (lane >= MLA_NOPE + half)
    m_cos = np.where(in_rope, np.cos(ang), 1.0)
    m_lo = np.where(first, -np.sin(ang), 0.0)
    m_hi = np.where(second, np.sin(ang), 0.0)
    return tuple(t.astype(np.float32) for t in (r_cos, r_lo, r_hi, m_cos, m_lo, m_hi))


def _pad_heads(w, heads, width):
    k = w.shape[0]
    w = w.reshape(k, heads, width)
    return jnp.pad(w, ((0, 0), (0, 0), (0, LANES - width))).reshape(k, heads * LANES)


def kernel(x, meta_tokens, norm_mix_w, w_in, ret_decay_fwd, ret_decay_bwd, ret_gn_w, w_ret_out, mla_q_norm_w, w_uq, mla_kv_norm_w, w_uk, w_uv, w_mla_out, w_o, norm_ffn_w, w_ffn_gate, w_ffn_up, w_ffn_down, norm_final_w):
    b, s, d = x.shape
    assert w_in.shape[0] == 1, "one layer: the meta rows are dropped after it"
    ret_qk_w = RET_HEADS * RET_DK
    ret_v_w = RET_HEADS * RET_DV
    q_rank = w_uq.shape[1]
    kv_rank = w_uk.shape[1]
    sizes = (ret_qk_w, ret_qk_w, ret_v_w, ret_v_w, q_rank, kv_rank, MLA_ROPE, d, d)
    assert sum(sizes) == w_in.shape[2]
    bounds = [0]
    for sz in sizes:
        bounds.append(bounds[-1] + sz)
    w_rq, w_rk, w_rv, w_rg, w_cq, w_ckv, w_kr, w_gr, w_gm = (
        w_in[0][:, lo:hi].astype(BF16) for lo, hi in zip(bounds[:-1], bounds[1:]))
    w_lat = jnp.concatenate(
        [w_cq, w_ckv, jnp.pad(w_kr, ((0, 0), (MLA_NOPE, LANES - MLA_QK)))], axis=1)
    row2 = lambda a: a.reshape(1, -1).astype(F32)
    wts1 = (w_rq, w_rk, w_rv, w_rg, w_lat, w_gr, w_gm,
            row2(mla_q_norm_w[0]), _pad_heads(w_uq[0], MLA_HEADS, MLA_QK).astype(BF16),
            row2(mla_kv_norm_w[0]), _pad_heads(w_uk[0], MLA_HEADS, MLA_NOPE).astype(BF16),
            _pad_heads(w_uv[0], MLA_HEADS, MLA_DV).astype(BF16))
    nw = row2(norm_mix_w[0])
    q_scale = (MLA_QK ** -0.5) * math.log2(math.e)

    tabs = _rope_tables(s + N_META)
    tabs_meta = tuple(t[:N_META] for t in tabs)
    tabs_x = tuple(t[N_META:] for t in tabs)
    rq, rk, rv, sg, q, k, v, gr, gm = _in_proj(x, tabs_x, nw, wts1, q_scale, tm=512)
    meta = meta_tokens.astype(x.dtype)[None]
    _, rk_m, rv_m, _, _, k_m, v_m, _, _ = _in_proj(meta, tabs_meta, nw, wts1, q_scale, tm=N_META)

    lgf = -jnp.exp(ret_decay_fwd[0].astype(F32))
    lgb = -jnp.exp(ret_decay_bwd[0].astype(F32))
    sb = _retention_bwd_states(lgb, rk, rv, cpb=16)
    y_att, y_ret = _attention_retention(
        lgf, lgb, q, k, v, k_m[0], v_m[0], rq, rk, rv, sg, sb, rk_m[0], rv_m[0],
        row2(ret_gn_w[0]), tq=1024, tk=2048)

    wts4 = (w_ret_out[0].astype(BF16), w_mla_out[0].astype(BF16), w_o[0].astype(BF16),
            row2(norm_ffn_w[0]), w_ffn_gate[0].astype(BF16), w_ffn_up[0].astype(BF16),
            w_ffn_down[0].astype(BF16), row2(norm_final_w))
    return _merge_ffn(x, y_ret, y_att, gr, gm, wts4, tm=512, ffn_chunks=2)
```

```python
import functools
import math

import jax
import jax.numpy as jnp
import numpy as np
from jax import lax
from jax.experimental import pallas as pl
from jax.experimental.pallas import tpu as pltpu

N_META = 16
CHUNK = 128
RET_HEADS = 8
RET_DK = 64
RET_DV = 128
MLA_HEADS = 8
MLA_NOPE = 64
MLA_ROPE = 32
MLA_DV = 64
MLA_QK = MLA_NOPE + MLA_ROPE
ROPE_BASE = 10000.0
RMS_EPS = 1e-6
GN_EPS = 1e-5

LANES = 128
MXU_TILE = 256
ONES_LANE = MLA_DV
VMEM_LIMIT = 56 * 1024 * 1024

F32 = jnp.float32
BF16 = jnp.bfloat16

_TN = (((0,), (0,)), ((), ()))
_NT = (((1,), (1,)), ((), ()))


def _rms(x, w):
    return x * lax.rsqrt(jnp.mean(x * x, axis=-1, keepdims=True) + RMS_EPS) * w


def _sigmoid(x):
    return 1.0 / (1.0 + jnp.exp(-x))


def _rope_group(x, cos, sin_lo, sin_hi, half):
    return (x * cos + pltpu.roll(x, LANES - half, axis=1) * sin_lo
            + pltpu.roll(x, half, axis=1) * sin_hi)


def _in_proj_kernel(x_ref, nw_ref, w_rq, w_rk, w_rv, w_rg, w_lat, w_gr, w_gm,
                    qnw_ref, w_uq, kvnw_ref, w_uk, w_uv,
                    rc_ref, rlo_ref, rhi_ref, mc_ref, mlo_ref, mhi_ref, lgb_ref,
                    rq_o, rk_o, rv_o, sg_o, q_o, k_o, v_o, gr_o, gm_o, *rest, q_scale):
    if rest:
        sb_o, sb_run, wkb, gcb = rest

        @pl.when(pl.program_id(1) == 0)
        def _start_of_sequence():
            for i in range(N_PAIRS):
                wkb[i] = jnp.exp(_pair_scalar(lgb_ref, i, _odd_lane()) * _chunk_pos())
                gcb[i] = jnp.exp(_pair_scalar(lgb_ref, i, _odd_col(8)) * CHUNK)
                sb_run[i] = jnp.zeros((LANES, PAIR_W), F32)

    u = _rms(x_ref[0], nw_ref[...]).astype(BF16)

    def mm(a, w):
        return jnp.dot(a, w[...], preferred_element_type=F32)

    rc, rlo, rhi = rc_ref[...], rlo_ref[...], rhi_ref[...]
    mc, mlo, mhi = mc_ref[...], mlo_ref[...], mhi_ref[...]

    lat = mm(u, w_lat)
    q_rank = qnw_ref.shape[1]
    kv_rank = kvnw_ref.shape[1]
    cq = _rms(lat[:, :q_rank], qnw_ref[...]).astype(BF16)
    ckv = _rms(lat[:, q_rank:q_rank + kv_rank], kvnw_ref[...]).astype(BF16)
    kr = _rope_group(lat[:, q_rank + kv_rank:], mc, mlo, mhi, MLA_ROPE // 2)

    rk = mm(u, w_rk)
    rk_groups = []
    for c in range(rk.shape[1] // LANES):
        sl = slice(c * LANES, (c + 1) * LANES)
        rk_groups.append((_rope_group(rk[:, sl], rc, rlo, rhi, RET_DK // 2)
                          * (RET_DK ** -0.5)).astype(BF16))
        rk_o[0, :, sl] = rk_groups[c]
    rv = mm(u, w_rv).astype(BF16)
    rv_o[0] = rv

    if rest:
        for i in range(N_PAIRS):
            state = sb_run[i]
            for c in range(x_ref.shape[1] // CHUNK - 1, -1, -1):
                rows = slice(c * CHUNK, (c + 1) * CHUNK)
                sb_o[0, c, i] = state.astype(BF16)
                inc = _state_increment(rk_groups[i][rows], wkb[i],
                                       rv[rows, i * PAIR_W:(i + 1) * PAIR_W])
                state = gcb[i][0:1, :] * state + inc
            sb_run[i] = state

    q = mm(cq, w_uq)
    kn = mm(ckv, w_uk)
    for h in range(MLA_HEADS):
        sl = slice(h * LANES, (h + 1) * LANES)
        q_o[0, :, sl] = (_rope_group(q[:, sl], mc, mlo, mhi, MLA_ROPE // 2) * q_scale).astype(BF16)
        k_o[0, :, sl] = (kn[:, sl] + kr).astype(BF16)
    v = mm(ckv, w_uv)
    lane = lax.broadcasted_iota(jnp.int32, v.shape, 1)
    v_o[0] = jnp.where(lane % LANES == ONES_LANE, 1.0, v).astype(BF16)

    rq = mm(u, w_rq)
    for c in range(rq.shape[1] // LANES):
        sl = slice(c * LANES, (c + 1) * LANES)
        rq_o[0, :, sl] = _rope_group(rq[:, sl], rc, rlo, rhi, RET_DK // 2).astype(BF16)

    rg = mm(u, w_rg)
    sg_o[0] = (rg * _sigmoid(rg)).astype(BF16)
    gr_o[0] = _sigmoid(mm(u, w_gr)).astype(BF16)
    gm_o[0] = _sigmoid(mm(u, w_gm)).astype(BF16)


def _in_proj(rows, tabs, nw, wts, lgb, q_scale, tm, with_bwd_states):
    nb, r, d = rows.shape
    n_tiles = r // tm
    const = lambda b, i: (0, 0)

    def wspec(w):
        return pl.BlockSpec(w.shape, const, pipeline_mode=pl.Buffered(1))

    row_spec = lambda width: pl.BlockSpec((1, tm, width), lambda b, i: (b, n_tiles - 1 - i, 0))
    tab_spec = pl.BlockSpec((tm, LANES), lambda b, i: (n_tiles - 1 - i, 0))
    (w_rq, w_rk, w_rv, w_rg, w_lat, w_gr, w_gm, qnw, w_uq, kvnw, w_uk, w_uv) = wts
    out_widths = (w_rq.shape[1], w_rk.shape[1], w_rv.shape[1], w_rg.shape[1],
                  w_uq.shape[1], w_uk.shape[1], w_uv.shape[1], w_gr.shape[1], w_gm.shape[1])
    out_specs = [row_spec(w) for w in out_widths]
    out_shape = [jax.ShapeDtypeStruct((nb, r, w), BF16) for w in out_widths]
    scratch = []
    if with_bwd_states:
        cpb = tm // CHUNK
        out_specs.append(pl.BlockSpec((1, cpb, N_PAIRS, LANES, PAIR_W),
                                      lambda b, i: (b, n_tiles - 1 - i, 0, 0, 0)))
        out_shape.append(jax.ShapeDtypeStruct((nb, r // CHUNK, N_PAIRS, LANES, PAIR_W), BF16))
        scratch = [
            pltpu.VMEM((N_PAIRS, LANES, PAIR_W), F32),
            pltpu.VMEM((N_PAIRS, CHUNK, LANES), F32),
            pltpu.VMEM((N_PAIRS, 8, PAIR_W), F32),
        ]
    return pl.pallas_call(
        functools.partial(_in_proj_kernel, q_scale=q_scale),
        grid=(nb, n_tiles),
        in_specs=([row_spec(d), wspec(nw)] + [wspec(w) for w in wts] + [tab_spec] * 6
                  + [pl.BlockSpec(memory_space=pltpu.SMEM)]),
        out_specs=out_specs,
        out_shape=out_shape,
        scratch_shapes=scratch,
        compiler_params=pltpu.CompilerParams(
            dimension_semantics=("arbitrary", "arbitrary"), vmem_limit_bytes=VMEM_LIMIT),
        name="in_proj",
    )(rows, nw, *wts, *tabs, lgb)


N_PAIRS = RET_HEADS // 2
PAIR_W = 2 * RET_DV


def _odd_lane():
    return lax.broadcasted_iota(jnp.int32, (CHUNK, LANES), 1) >= RET_DK


def _odd_col(rows=CHUNK):
    return lax.broadcasted_iota(jnp.int32, (rows, PAIR_W), 1) >= RET_DV


def _own_head():
    row = lax.broadcasted_iota(jnp.int32, (LANES, PAIR_W), 0)
    return (row >= RET_DK) == _odd_col(LANES)


def _chunk_pos():
    return lax.broadcasted_iota(jnp.int32, (CHUNK, LANES), 0).astype(F32)


def _pair_scalar(ref, i, odd):
    return jnp.where(odd, ref[2 * i + 1], ref[2 * i])


def _state_increment(kp, w, vp):
    kw = (kp.astype(F32) * w).astype(BF16)
    kv = lax.dot_general(kw, vp, _TN, preferred_element_type=F32)
    return jnp.where(_own_head(), kv, 0.0)


def _retention_tables(lgf_ref, lgb_ref, dmask, wqf, wqb, wkf, gcf):
    pos = _chunk_pos()
    row2 = lax.broadcasted_iota(jnp.int32, (CHUNK, PAIR_W), 0)
    col2 = lax.broadcasted_iota(jnp.int32, (CHUNK, PAIR_W), 1)
    odd_col = _odd_col()
    rel = (row2 - jnp.where(odd_col, col2 - RET_DV, col2)).astype(F32)
    for i in range(N_PAIRS):
        lf = _pair_scalar(lgf_ref, i, _odd_lane())
        lb = _pair_scalar(lgb_ref, i, _odd_lane())
        lf2 = _pair_scalar(lgf_ref, i, odd_col)
        lb2 = _pair_scalar(lgb_ref, i, odd_col)
        dmask[i] = jnp.exp(jnp.where(rel >= 0, lf2 * rel, -lb2 * rel))
        wqf[i] = jnp.exp(lf * (pos + 1.0))
        wqb[i] = jnp.exp(lb * (CHUNK - pos))
        wkf[i] = jnp.exp(lf * (CHUNK - 1.0 - pos))
        gcf[i] = jnp.exp(_pair_scalar(lgf_ref, i, _odd_col(8)) * CHUNK)


def _retention_meta_state(lgf_ref, km_ref, vm_ref, sf_run):
    mrow = lax.broadcasted_iota(jnp.int32, (N_META, LANES), 0).astype(F32)
    odd_m = lax.broadcasted_iota(jnp.int32, (N_META, LANES), 1) >= RET_DK
    for i in range(N_PAIRS):
        wm = jnp.exp(_pair_scalar(lgf_ref, i, odd_m) * (N_META - 1.0 - mrow))
        kmw = (km_ref[:, i * LANES:(i + 1) * LANES].astype(F32) * wm).astype(BF16)
        kv = lax.dot_general(kmw, vm_ref[:, i * PAIR_W:(i + 1) * PAIR_W], _TN,
                             preferred_element_type=F32)
        sf_run[i] = jnp.where(_own_head(), kv, 0.0)


def _retention_chunk(q_ref, k_ref, v_ref, sg_ref, sb_ref, gnw_ref, o_ref,
                     sf_run, dmask, wqf, wqb, wkf, gcf):
    zeros = jnp.zeros((CHUNK, RET_DV), BF16)
    odd_lane = _odd_lane()
    for i in range(N_PAIRS):
        qb16 = q_ref[0, :, i * LANES:(i + 1) * LANES]
        kp = k_ref[0, :, i * LANES:(i + 1) * LANES]
        vp = v_ref[0, :, i * PAIR_W:(i + 1) * PAIR_W]
        qp = qb16.astype(F32)
        kpf = kp.astype(F32)
        k2 = jnp.concatenate([jnp.where(odd_lane, 0.0, kpf).astype(BF16),
                              jnp.where(odd_lane, kpf, 0.0).astype(BF16)], axis=0)
        s2 = lax.dot_general(qb16, k2, _NT, preferred_element_type=F32)
        v_diag = jnp.concatenate([
            jnp.concatenate([vp[:, :RET_DV], zeros], axis=1),
            jnp.concatenate([zeros, vp[:, RET_DV:]], axis=1)], axis=0)
        q_cross = jnp.concatenate([(qp * wqf[i]).astype(BF16),
                                   (qp * wqb[i]).astype(BF16)], axis=1)
        state = sf_run[i]
        states = jnp.concatenate([state.astype(BF16), sb_ref[0, 0, i]], axis=0)
        o2 = (jnp.dot((s2 * dmask[i]).astype(BF16), v_diag, preferred_element_type=F32)
              + jnp.dot(q_cross, states, preferred_element_type=F32))
        sf_run[i] = gcf[i][0:1, :] * state + _state_increment(kp, wkf[i], vp)
        for j in range(2):
            o = o2[:, j * RET_DV:(j + 1) * RET_DV]
            mu = jnp.mean(o, axis=-1, keepdims=True)
            d = o - mu
            var = jnp.mean(d * d, axis=-1, keepdims=True)
            yn = d * lax.rsqrt(var + GN_EPS)
            hs = slice((2 * i + j) * RET_DV, (2 * i + j + 1) * RET_DV)
            y = yn * gnw_ref[:, hs] * sg_ref[0, :, hs].astype(F32)
            o_ref[0, :, hs] = y.astype(BF16)


def _attn_ret_kernel(lgf_ref, lgb_ref, q_ref, k_ref, v_ref, km_ref, vm_ref,
                     rq_ref, rk_ref, rv_ref, sg_ref, sb_ref, rkm_ref, rvm_ref, gnw_ref,
                     o_ref, yr_ref, sf_run, dmask, wqf, wqb, wkf, gcf, *, tk):
    @pl.when((pl.program_id(1) == 0) & (pl.program_id(2) == 0))
    def _start_of_sequence():
        _retention_tables(lgf_ref, lgb_ref, dmask, wqf, wqb, wkf, gcf)
        _retention_meta_state(lgf_ref, rkm_ref, rvm_ref, sf_run)

    _retention_chunk(rq_ref, rk_ref, rv_ref, sg_ref, sb_ref, gnw_ref, yr_ref,
                     sf_run, dmask, wqf, wqb, wkf, gcf)

    q = q_ref[0]
    s0 = lax.dot_general(q, km_ref[...], _NT, preferred_element_type=F32)
    m0 = jnp.max(s0, axis=1, keepdims=True)
    acc0 = jnp.dot(jnp.exp2(s0 - m0).astype(BF16), vm_ref[...], preferred_element_type=F32)

    m, acc = m0, acc0
    for j in range(k_ref.shape[1] // tk):
        kt = k_ref[0, j * tk:(j + 1) * tk, :]
        vt = v_ref[0, j * tk:(j + 1) * tk, :]
        s = lax.dot_general(q, kt, _NT, preferred_element_type=F32)
        m_new = jnp.maximum(m, jnp.max(s, axis=1, keepdims=True))
        p = jnp.exp2(s - m_new).astype(BF16)
        acc = jnp.exp2(m - m_new) * acc + jnp.dot(p, vt, preferred_element_type=F32)
        m = m_new
    o_ref[0] = (acc / acc[:, ONES_LANE:ONES_LANE + 1]).astype(BF16)


def _attention_retention(lgf, lgb, q, k, v, km, vm, rq, rk, rv, sg, sb, rkm, rvm, gnw, tq, tk):
    b, s, w = q.shape
    heads = w // LANES
    n_q = s // tq
    assert heads * n_q == s // CHUNK, "one retention chunk per attention grid step"
    v_w = rv.shape[2]
    att = lambda bi, h, i, *_: (bi, i, h)
    kv = lambda bi, h, i, *_: (bi, 0, h)
    meta = lambda bi, h, i, *_: (0, h)
    chunk = lambda bi, h, i, *_: (bi, h * n_q + i, 0)
    const = lambda bi, h, i, *_: (0, 0)
    tab = lambda rows, width: pltpu.VMEM((N_PAIRS, rows, width), F32)
    grid_spec = pltpu.PrefetchScalarGridSpec(
        num_scalar_prefetch=2,
        grid=(b, heads, n_q),
        in_specs=[
            pl.BlockSpec((1, tq, LANES), att),
            pl.BlockSpec((1, s, LANES), kv),
            pl.BlockSpec((1, s, LANES), kv),
            pl.BlockSpec((N_META, LANES), meta),
            pl.BlockSpec((N_META, LANES), meta),
            pl.BlockSpec((1, CHUNK, rq.shape[2]), chunk),
            pl.BlockSpec((1, CHUNK, rk.shape[2]), chunk),
            pl.BlockSpec((1, CHUNK, v_w), chunk),
            pl.BlockSpec((1, CHUNK, v_w), chunk),
            pl.BlockSpec((1, 1, N_PAIRS, LANES, PAIR_W),
                         lambda bi, h, i, *_: (bi, h * n_q + i, 0, 0, 0)),
            pl.BlockSpec(rkm.shape, const),
            pl.BlockSpec(rvm.shape, const),
            pl.BlockSpec(gnw.shape, const),
        ],
        out_specs=[pl.BlockSpec((1, tq, LANES), att), pl.BlockSpec((1, CHUNK, v_w), chunk)],
        scratch_shapes=[
            tab(LANES, PAIR_W),
            tab(CHUNK, PAIR_W),
            tab(CHUNK, LANES), tab(CHUNK, LANES), tab(CHUNK, LANES),
            tab(8, PAIR_W),
        ],
    )
    return pl.pallas_call(
        functools.partial(_attn_ret_kernel, tk=tk),
        grid_spec=grid_spec,
        out_shape=[jax.ShapeDtypeStruct((b, s, w), BF16), jax.ShapeDtypeStruct((b, s, v_w), BF16)],
        compiler_params=pltpu.CompilerParams(
            dimension_semantics=("arbitrary", "arbitrary", "arbitrary"),
            vmem_limit_bytes=VMEM_LIMIT),
        name="attention_retention",
    )(lgf, lgb, q, k, v, km, vm, rq, rk, rv, sg, sb, rkm, rvm, gnw)


def _merge_ffn_kernel(x_ref, yr_ref, ya_ref, gr_ref, gm_ref, w_ro, w_mo, w_o, nfw_ref,
                      w_g, w_u, w_d, nfin_ref, o_ref, *, ffn_chunks):
    def mm(a, w):
        return jnp.dot(a, w[...], preferred_element_type=F32)

    lane = lax.broadcasted_iota(jnp.int32, (ya_ref.shape[1], LANES), 1)
    packed = []
    for i in range(MLA_HEADS // 2):
        even = ya_ref[0, :, (2 * i) * LANES:(2 * i + 1) * LANES].astype(F32)
        odd = ya_ref[0, :, (2 * i + 1) * LANES:(2 * i + 2) * LANES].astype(F32)
        packed.append(jnp.where(lane < MLA_DV, even, pltpu.roll(odd, MLA_DV, axis=1)))
    ya = jnp.concatenate(packed, axis=1).astype(BF16)
    merged = (gr_ref[0].astype(F32) * mm(yr_ref[0], w_ro)
              + gm_ref[0].astype(F32) * mm(ya, w_mo))
    h1 = x_ref[0] + mm(merged.astype(BF16), w_o)
    u = _rms(h1, nfw_ref[...]).astype(BF16)
    h2 = h1
    n_tiles = w_g.shape[1] // MXU_TILE
    edges = [MXU_TILE * ((n_tiles * c + ffn_chunks - 1) // ffn_chunks) for c in range(ffn_chunks + 1)]
    for lo, hi in zip(edges[:-1], edges[1:]):
        cols = slice(lo, hi)
        g = jnp.dot(u, w_g[:, cols], preferred_element_type=F32)
        up = jnp.dot(u, w_u[:, cols], preferred_element_type=F32)
        act = (g * _sigmoid(g) * up).astype(BF16)
        h2 = h2 + jnp.dot(act, w_d[cols, :], preferred_element_type=F32)
    o_ref[0] = _rms(h2, nfin_ref[...])


def _merge_ffn(x, yr, ya, gr, gm, wts, tm, ffn_chunks):
    b, s, d = x.shape
    const = lambda bi, i: (0, 0)
    wspec = lambda w: pl.BlockSpec(w.shape, const, pipeline_mode=pl.Buffered(1))
    row = pl.BlockSpec((1, tm, d), lambda bi, i: (bi, i, 0))
    return pl.pallas_call(
        functools.partial(_merge_ffn_kernel, ffn_chunks=ffn_chunks),
        grid=(b, s // tm),
        in_specs=[row] * 5 + [wspec(w) for w in wts],
        out_specs=row,
        out_shape=jax.ShapeDtypeStruct((b, s, d), F32),
        compiler_params=pltpu.CompilerParams(
            dimension_semantics=("arbitrary", "arbitrary"), vmem_limit_bytes=VMEM_LIMIT),
        name="merge_ffn",
    )(x, yr, ya, gr, gm, *wts)


def _rope_tables(n_pos):
    pos = np.arange(n_pos, dtype=np.float64)[:, None]
    lane = np.arange(LANES)

    def angles(half):
        inv = ROPE_BASE ** (-np.arange(half, dtype=np.float64) / half)
        return pos * inv[None, :]

    half = RET_DK // 2
    ang = angles(half)[:, lane % half]
    first = (lane % RET_DK) < half
    r_cos = np.cos(ang)
    r_lo = np.where(first, -np.sin(ang), 0.0)
    r_hi = np.where(first, 0.0, np.sin(ang))
    half = MLA_ROPE // 2
    ang = angles(half)[:, lane % half]
    in_rope = (lane >= MLA_NOPE) & (lane < MLA_QK)
    first = in_rope & (lane < MLA_NOPE + half)
    second = in_rope & (lane >= MLA_NOPE + half)
    m_cos = np.where(in_rope, np.cos(ang), 1.0)
    m_lo = np.where(first, -np.sin(ang), 0.0)
    m_hi = np.where(second, np.sin(ang), 0.0)
    return tuple(t.astype(np.float32) for t in (r_cos, r_lo, r_hi, m_cos, m_lo, m_hi))


def _pad_heads(w, heads, width):
    k = w.shape[0]
    w = w.reshape(k, heads, width)
    return jnp.pad(w, ((0, 0), (0, 0), (0, LANES - width))).reshape(k, heads * LANES)


def kernel(x, meta_tokens, norm_mix_w, w_in, ret_decay_fwd, ret_decay_bwd, ret_gn_w, w_ret_out, mla_q_norm_w, w_uq, mla_kv_norm_w, w_uk, w_uv, w_mla_out, w_o, norm_ffn_w, w_ffn_gate, w_ffn_up, w_ffn_down, norm_final_w):
    b, s, d = x.shape
    assert w_in.shape[0] == 1, "one layer: the meta rows are dropped after it"
    ret_qk_w = RET_HEADS * RET_DK
    ret_v_w = RET_HEADS * RET_DV
    q_rank = w_uq.shape[1]
    kv_rank = w_uk.shape[1]
    sizes = (ret_qk_w, ret_qk_w, ret_v_w, ret_v_w, q_rank, kv_rank, MLA_ROPE, d, d)
    assert sum(sizes) == w_in.shape[2]
    bounds = [0]
    for sz in sizes:
        bounds.append(bounds[-1] + sz)
    w_rq, w_rk, w_rv, w_rg, w_cq, w_ckv, w_kr, w_gr, w_gm = (
        w_in[0][:, lo:hi].astype(BF16) for lo, hi in zip(bounds[:-1], bounds[1:]))
    w_lat = jnp.concatenate(
        [w_cq, w_ckv, jnp.pad(w_kr, ((0, 0), (MLA_NOPE, LANES - MLA_QK)))], axis=1)
    row2 = lambda a: a.reshape(1, -1).astype(F32)
    wts1 = (w_rq, w_rk, w_rv, w_rg, w_lat, w_gr, w_gm,
            row2(mla_q_norm_w[0]), _pad_heads(w_uq[0], MLA_HEADS, MLA_QK).astype(BF16),
            row2(mla_kv_norm_w[0]), _pad_heads(w_uk[0], MLA_HEADS, MLA_NOPE).astype(BF16),
            _pad_heads(w_uv[0], MLA_HEADS, MLA_DV).astype(BF16))
    nw = row2(norm_mix_w[0])
    q_scale = (MLA_QK ** -0.5) * math.log2(math.e)

    tabs = _rope_tables(s + N_META)
    tabs_meta = tuple(t[:N_META] for t in tabs)
    tabs_x = tuple(t[N_META:] for t in tabs)
    lgf = -jnp.exp(ret_decay_fwd[0].astype(F32))
    lgb = -jnp.exp(ret_decay_bwd[0].astype(F32))
    rq, rk, rv, sg, q, k, v, gr, gm, sb = _in_proj(
        x, tabs_x, nw, wts1, lgb, q_scale, tm=512, with_bwd_states=True)
    meta = meta_tokens.astype(x.dtype)[None]
    _, rk_m, rv_m, _, _, k_m, v_m, _, _ = _in_proj(
        meta, tabs_meta, nw, wts1, lgb, q_scale, tm=N_META, with_bwd_states=False)
    y_att, y_ret = _attention_retention(
        lgf, lgb, q, k, v, k_m[0], v_m[0], rq, rk, rv, sg, sb, rk_m[0], rv_m[0],
        row2(ret_gn_w[0]), tq=1024, tk=2048)

    wts4 = (w_ret_out[0].astype(BF16), w_mla_out[0].astype(BF16), w_o[0].astype(BF16),
            row2(norm_ffn_w[0]), w_ffn_gate[0].astype(BF16), w_ffn_up[0].astype(BF16),
            w_ffn_down[0].astype(BF16), row2(norm_final_w))
    return _merge_ffn(x, y_ret, y_att, gr, gm, wts4, tm=512, ffn_chunks=2)
```

```python
import functools
import math

import jax
import jax.numpy as jnp
import numpy as np
from jax import lax
from jax.experimental import pallas as pl
from jax.experimental.pallas import tpu as pltpu

N_META = 16
CHUNK = 128
RET_HEADS = 8
RET_DK = 64
RET_DV = 128
MLA_HEADS = 8
MLA_NOPE = 64
MLA_ROPE = 32
MLA_DV = 64
MLA_QK = MLA_NOPE + MLA_ROPE
ROPE_BASE = 10000.0
RMS_EPS = 1e-6
GN_EPS = 1e-5

LANES = 128
MXU_TILE = 256
ONES_LANE = MLA_DV
VMEM_LIMIT = 56 * 1024 * 1024

IN_PROJ_ROWS = 512
ATTN_Q_ROWS = 1024
ATTN_KEY_TILE = 2048
MERGE_ROWS = 512
FFN_CHUNKS = 4

F32 = jnp.float32
BF16 = jnp.bfloat16

_TN = (((0,), (0,)), ((), ()))
_NT = (((1,), (1,)), ((), ()))


def _rms(x, w):
    return x * lax.rsqrt(jnp.mean(x * x, axis=-1, keepdims=True) + RMS_EPS) * w


def _sigmoid(x):
    return 1.0 / (1.0 + jnp.exp(-x))


def _rope_group(x, cos, sin_lo, sin_hi, half):
    return (x * cos + pltpu.roll(x, LANES - half, axis=1) * sin_lo
            + pltpu.roll(x, half, axis=1) * sin_hi)


def _in_proj_kernel(x_ref, nw_ref, w_rq, w_rk, w_rv, w_rg, w_lat, w_gr, w_gm,
                    qnw_ref, w_uq, kvnw_ref, w_uk, w_uv,
                    rc_ref, rlo_ref, rhi_ref, mc_ref, mlo_ref, mhi_ref, lgb_ref,
                    rq_o, rk_o, rv_o, sg_o, q_o, k_o, v_o, gr_o, gm_o, *rest, q_scale):
    if rest:
        sb_o, sb_run, wkb, gcb = rest

        @pl.when(pl.program_id(1) == 0)
        def _start_of_sequence():
            for i in range(N_PAIRS):
                wkb[i] = jnp.exp(_pair_scalar(lgb_ref, i, _odd_lane()) * _chunk_pos())
                gcb[i] = jnp.exp(_pair_scalar(lgb_ref, i, _odd_col(8)) * CHUNK)
                sb_run[i] = jnp.zeros((LANES, PAIR_W), F32)

    u = _rms(x_ref[0], nw_ref[...]).astype(BF16)

    def mm(a, w):
        return jnp.dot(a, w[...], preferred_element_type=F32)

    rc, rlo, rhi = rc_ref[...], rlo_ref[...], rhi_ref[...]
    mc, mlo, mhi = mc_ref[...], mlo_ref[...], mhi_ref[...]

    lat = mm(u, w_lat)
    q_rank = qnw_ref.shape[1]
    kv_rank = kvnw_ref.shape[1]
    cq = _rms(lat[:, :q_rank], qnw_ref[...]).astype(BF16)
    ckv = _rms(lat[:, q_rank:q_rank + kv_rank], kvnw_ref[...]).astype(BF16)
    kr = _rope_group(lat[:, q_rank + kv_rank:], mc, mlo, mhi, MLA_ROPE // 2)

    rk = mm(u, w_rk)
    rk_groups = []
    for c in range(rk.shape[1] // LANES):
        sl = slice(c * LANES, (c + 1) * LANES)
        rk_groups.append((_rope_group(rk[:, sl], rc, rlo, rhi, RET_DK // 2)
                          * (RET_DK ** -0.5)).astype(BF16))
        rk_o[0, :, sl] = rk_groups[c]
    rv = mm(u, w_rv).astype(BF16)
    rv_o[0] = rv

    if rest:
        for i in range(N_PAIRS):
            state = sb_run[i]
            for c in range(x_ref.shape[1] // CHUNK - 1, -1, -1):
                rows = slice(c * CHUNK, (c + 1) * CHUNK)
                sb_o[0, c, i] = state.astype(BF16)
                inc = _state_increment(rk_groups[i][rows], wkb[i],
                                       rv[rows, i * PAIR_W:(i + 1) * PAIR_W])
                state = gcb[i][0:1, :] * state + inc
            sb_run[i] = state

    q = mm(cq, w_uq)
    kn = mm(ckv, w_uk)
    for h in range(MLA_HEADS):
        sl = slice(h * LANES, (h + 1) * LANES)
        q_o[0, :, sl] = (_rope_group(q[:, sl], mc, mlo, mhi, MLA_ROPE // 2) * q_scale).astype(BF16)
        k_o[0, :, sl] = (kn[:, sl] + kr).astype(BF16)
    v = mm(ckv, w_uv)
    lane = lax.broadcasted_iota(jnp.int32, v.shape, 1)
    v_o[0] = jnp.where(lane % LANES == ONES_LANE, 1.0, v).astype(BF16)

    rq = mm(u, w_rq)
    for c in range(rq.shape[1] // LANES):
        sl = slice(c * LANES, (c + 1) * LANES)
        rq_o[0, :, sl] = _rope_group(rq[:, sl], rc, rlo, rhi, RET_DK // 2).astype(BF16)

    rg = mm(u, w_rg)
    sg_o[0] = (rg * _sigmoid(rg)).astype(BF16)
    gr_o[0] = _sigmoid(mm(u, w_gr)).astype(BF16)
    gm_o[0] = _sigmoid(mm(u, w_gm)).astype(BF16)


def _in_proj(rows, tabs, nw, wts, lgb, q_scale, tm, with_bwd_states):
    nb, r, d = rows.shape
    n_tiles = r // tm
    const = lambda b, i: (0, 0)

    def wspec(w):
        return pl.BlockSpec(w.shape, const, pipeline_mode=pl.Buffered(1))

    row_spec = lambda width: pl.BlockSpec((1, tm, width), lambda b, i: (b, n_tiles - 1 - i, 0))
    tab_spec = pl.BlockSpec((tm, LANES), lambda b, i: (n_tiles - 1 - i, 0))
    (w_rq, w_rk, w_rv, w_rg, w_lat, w_gr, w_gm, qnw, w_uq, kvnw, w_uk, w_uv) = wts
    out_widths = (w_rq.shape[1], w_rk.shape[1], w_rv.shape[1], w_rg.shape[1],
                  w_uq.shape[1], w_uk.shape[1], w_uv.shape[1], w_gr.shape[1], w_gm.shape[1])
    out_specs = [row_spec(w) for w in out_widths]
    out_shape = [jax.ShapeDtypeStruct((nb, r, w), BF16) for w in out_widths]
    scratch = []
    if with_bwd_states:
        cpb = tm // CHUNK
        out_specs.append(pl.BlockSpec((1, cpb, N_PAIRS, LANES, PAIR_W),
                                      lambda b, i: (b, n_tiles - 1 - i, 0, 0, 0)))
        out_shape.append(jax.ShapeDtypeStruct((nb, r // CHUNK, N_PAIRS, LANES, PAIR_W), BF16))
        scratch = [
            pltpu.VMEM((N_PAIRS, LANES, PAIR_W), F32),
            pltpu.VMEM((N_PAIRS, CHUNK, LANES), F32),
            pltpu.VMEM((N_PAIRS, 8, PAIR_W), F32),
        ]
    return pl.pallas_call(
        functools.partial(_in_proj_kernel, q_scale=q_scale),
        grid=(nb, n_tiles),
        in_specs=([row_spec(d), wspec(nw)] + [wspec(w) for w in wts] + [tab_spec] * 6
                  + [pl.BlockSpec(memory_space=pltpu.SMEM)]),
        out_specs=out_specs,
        out_shape=out_shape,
        scratch_shapes=scratch,
        compiler_params=pltpu.CompilerParams(
            dimension_semantics=("arbitrary", "arbitrary"), vmem_limit_bytes=VMEM_LIMIT),
        name="in_proj",
    )(rows, nw, *wts, *tabs, lgb)


N_PAIRS = RET_HEADS // 2
PAIR_W = 2 * RET_DV


def _odd_lane():
    return lax.broadcasted_iota(jnp.int32, (CHUNK, LANES), 1) >= RET_DK


def _odd_col(rows=CHUNK):
    return lax.broadcasted_iota(jnp.int32, (rows, PAIR_W), 1) >= RET_DV


def _own_head():
    row = lax.broadcasted_iota(jnp.int32, (LANES, PAIR_W), 0)
    return (row >= RET_DK) == _odd_col(LANES)


def _chunk_pos():
    return lax.broadcasted_iota(jnp.int32, (CHUNK, LANES), 0).astype(F32)


def _pair_scalar(ref, i, odd):
    return jnp.where(odd, ref[2 * i + 1], ref[2 * i])


def _state_increment(kp, w, vp):
    kw = (kp.astype(F32) * w).astype(BF16)
    kv = lax.dot_general(kw, vp, _TN, preferred_element_type=F32)
    return jnp.where(_own_head(), kv, 0.0)


def _retention_tables(lgf_ref, lgb_ref, dmask, wqf, wqb, wkf, gcf):
    pos = _chunk_pos()
    row2 = lax.broadcasted_iota(jnp.int32, (CHUNK, PAIR_W), 0)
    col2 = lax.broadcasted_iota(jnp.int32, (CHUNK, PAIR_W), 1)
    odd_col = _odd_col()
    rel = (row2 - jnp.where(odd_col, col2 - RET_DV, col2)).astype(F32)
    for i in range(N_PAIRS):
        lf = _pair_scalar(lgf_ref, i, _odd_lane())
        lb = _pair_scalar(lgb_ref, i, _odd_lane())
        lf2 = _pair_scalar(lgf_ref, i, odd_col)
        lb2 = _pair_scalar(lgb_ref, i, odd_col)
        dmask[i] = jnp.exp(jnp.where(rel >= 0, lf2 * rel, -lb2 * rel))
        wqf[i] = jnp.exp(lf * (pos + 1.0))
        wqb[i] = jnp.exp(lb * (CHUNK - pos))
        wkf[i] = jnp.exp(lf * (CHUNK - 1.0 - pos))
        gcf[i] = jnp.exp(_pair_scalar(lgf_ref, i, _odd_col(8)) * CHUNK)


def _retention_meta_state(lgf_ref, km_ref, vm_ref, sf_run):
    mrow = lax.broadcasted_iota(jnp.int32, (N_META, LANES), 0).astype(F32)
    odd_m = lax.broadcasted_iota(jnp.int32, (N_META, LANES), 1) >= RET_DK
    for i in range(N_PAIRS):
        wm = jnp.exp(_pair_scalar(lgf_ref, i, odd_m) * (N_META - 1.0 - mrow))
        kmw = (km_ref[:, i * LANES:(i + 1) * LANES].astype(F32) * wm).astype(BF16)
        kv = lax.dot_general(kmw, vm_ref[:, i * PAIR_W:(i + 1) * PAIR_W], _TN,
                             preferred_element_type=F32)
        sf_run[i] = jnp.where(_own_head(), kv, 0.0)


def _retention_chunk(q_ref, k_ref, v_ref, sg_ref, sb_ref, gnw_ref, o_ref,
                     sf_run, dmask, wqf, wqb, wkf, gcf):
    zeros = jnp.zeros((CHUNK, RET_DV), BF16)
    odd_lane = _odd_lane()
    for i in range(N_PAIRS):
        qb16 = q_ref[0, :, i * LANES:(i + 1) * LANES]
        kp = k_ref[0, :, i * LANES:(i + 1) * LANES]
        vp = v_ref[0, :, i * PAIR_W:(i + 1) * PAIR_W]
        qp = qb16.astype(F32)
        kpf = kp.astype(F32)
        k2 = jnp.concatenate([jnp.where(odd_lane, 0.0, kpf).astype(BF16),
                              jnp.where(odd_lane, kpf, 0.0).astype(BF16)], axis=0)
        s2 = lax.dot_general(qb16, k2, _NT, preferred_element_type=F32)
        v_diag = jnp.concatenate([
            jnp.concatenate([vp[:, :RET_DV], zeros], axis=1),
            jnp.concatenate([zeros, vp[:, RET_DV:]], axis=1)], axis=0)
        q_cross = jnp.concatenate([(qp * wqf[i]).astype(BF16),
                                   (qp * wqb[i]).astype(BF16)], axis=1)
        state = sf_run[i]
        states = jnp.concatenate([state.astype(BF16), sb_ref[0, 0, i]], axis=0)
        o2 = (jnp.dot((s2 * dmask[i]).astype(BF16), v_diag, preferred_element_type=F32)
              + jnp.dot(q_cross, states, preferred_element_type=F32))
        sf_run[i] = gcf[i][0:1, :] * state + _state_increment(kp, wkf[i], vp)
        for j in range(2):
            o = o2[:, j * RET_DV:(j + 1) * RET_DV]
            mu = jnp.mean(o, axis=-1, keepdims=True)
            d = o - mu
            var = jnp.mean(d * d, axis=-1, keepdims=True)
            yn = d * lax.rsqrt(var + GN_EPS)
            hs = slice((2 * i + j) * RET_DV, (2 * i + j + 1) * RET_DV)
            y = yn * gnw_ref[:, hs] * sg_ref[0, :, hs].astype(F32)
            o_ref[0, :, hs] = y.astype(BF16)


def _attn_ret_kernel(lgf_ref, lgb_ref, q_ref, k_ref, v_ref, km_ref, vm_ref,
                     rq_ref, rk_ref, rv_ref, sg_ref, sb_ref, rkm_ref, rvm_ref, gnw_ref,
                     o_ref, yr_ref, sf_run, dmask, wqf, wqb, wkf, gcf, *, tk):
    @pl.when((pl.program_id(1) == 0) & (pl.program_id(2) == 0))
    def _start_of_sequence():
        _retention_tables(lgf_ref, lgb_ref, dmask, wqf, wqb, wkf, gcf)
        _retention_meta_state(lgf_ref, rkm_ref, rvm_ref, sf_run)

    _retention_chunk(rq_ref, rk_ref, rv_ref, sg_ref, sb_ref, gnw_ref, yr_ref,
                     sf_run, dmask, wqf, wqb, wkf, gcf)

    q = q_ref[0]
    s0 = lax.dot_general(q, km_ref[...], _NT, preferred_element_type=F32)
    m0 = jnp.max(s0, axis=1, keepdims=True)
    acc0 = jnp.dot(jnp.exp2(s0 - m0).astype(BF16), vm_ref[...], preferred_element_type=F32)

    m, acc = m0, acc0
    for j in range(k_ref.shape[1] // tk):
        kt = k_ref[0, j * tk:(j + 1) * tk, :]
        vt = v_ref[0, j * tk:(j + 1) * tk, :]
        s = lax.dot_general(q, kt, _NT, preferred_element_type=F32)
        m_new = jnp.maximum(m, jnp.max(s, axis=1, keepdims=True))
        p = jnp.exp2(s - m_new).astype(BF16)
        acc = jnp.exp2(m - m_new) * acc + jnp.dot(p, vt, preferred_element_type=F32)
        m = m_new
    o_ref[0] = (acc / acc[:, ONES_LANE:ONES_LANE + 1]).astype(BF16)


def _attention_retention(lgf, lgb, q, k, v, km, vm, rq, rk, rv, sg, sb, rkm, rvm, gnw, tq, tk):
    b, s, w = q.shape
    heads = w // LANES
    n_q = s // tq
    assert heads * n_q == s // CHUNK, "one retention chunk per attention grid step"
    v_w = rv.shape[2]
    att = lambda bi, h, i, *_: (bi, i, h)
    kv = lambda bi, h, i, *_: (bi, 0, h)
    meta = lambda bi, h, i, *_: (0, h)
    chunk = lambda bi, h, i, *_: (bi, h * n_q + i, 0)
    const = lambda bi, h, i, *_: (0, 0)
    tab = lambda rows, width: pltpu.VMEM((N_PAIRS, rows, width), F32)
    grid_spec = pltpu.PrefetchScalarGridSpec(
        num_scalar_prefetch=2,
        grid=(b, heads, n_q),
        in_specs=[
            pl.BlockSpec((1, tq, LANES), att),
            pl.BlockSpec((1, s, LANES), kv),
            pl.BlockSpec((1, s, LANES), kv),
            pl.BlockSpec((N_META, LANES), meta),
            pl.BlockSpec((N_META, LANES), meta),
            pl.BlockSpec((1, CHUNK, rq.shape[2]), chunk),
            pl.BlockSpec((1, CHUNK, rk.shape[2]), chunk),
            pl.BlockSpec((1, CHUNK, v_w), chunk),
            pl.BlockSpec((1, CHUNK, v_w), chunk),
            pl.BlockSpec((1, 1, N_PAIRS, LANES, PAIR_W),
                         lambda bi, h, i, *_: (bi, h * n_q + i, 0, 0, 0)),
            pl.BlockSpec(rkm.shape, const),
            pl.BlockSpec(rvm.shape, const),
            pl.BlockSpec(gnw.shape, const),
        ],
        out_specs=[pl.BlockSpec((1, tq, LANES), att), pl.BlockSpec((1, CHUNK, v_w), chunk)],
        scratch_shapes=[
            tab(LANES, PAIR_W),
            tab(CHUNK, PAIR_W),
            tab(CHUNK, LANES), tab(CHUNK, LANES), tab(CHUNK, LANES),
            tab(8, PAIR_W),
        ],
    )
    return pl.pallas_call(
        functools.partial(_attn_ret_kernel, tk=tk),
        grid_spec=grid_spec,
        out_shape=[jax.ShapeDtypeStruct((b, s, w), BF16), jax.ShapeDtypeStruct((b, s, v_w), BF16)],
        compiler_params=pltpu.CompilerParams(
            dimension_semantics=("arbitrary", "arbitrary", "arbitrary"),
            vmem_limit_bytes=VMEM_LIMIT),
        name="attention_retention",
    )(lgf, lgb, q, k, v, km, vm, rq, rk, rv, sg, sb, rkm, rvm, gnw)


def _merge_ffn_kernel(x_ref, yr_ref, ya_ref, gr_ref, gm_ref, w_ro, w_mo, w_o, nfw_ref,
                      w_g, w_u, w_d, nfin_ref, o_ref, *, ffn_chunks):
    def mm(a, w):
        return jnp.dot(a, w[...], preferred_element_type=F32)

    lane = lax.broadcasted_iota(jnp.int32, (ya_ref.shape[1], LANES), 1)
    packed = []
    for i in range(MLA_HEADS // 2):
        even = ya_ref[0, :, (2 * i) * LANES:(2 * i + 1) * LANES].astype(F32)
        odd = ya_ref[0, :, (2 * i + 1) * LANES:(2 * i + 2) * LANES].astype(F32)
        packed.append(jnp.where(lane < MLA_DV, even, pltpu.roll(odd, MLA_DV, axis=1)))
    ya = jnp.concatenate(packed, axis=1).astype(BF16)
    merged = (gr_ref[0].astype(F32) * mm(yr_ref[0], w_ro)
              + gm_ref[0].astype(F32) * mm(ya, w_mo))
    h1 = x_ref[0] + mm(merged.astype(BF16), w_o)
    u = _rms(h1, nfw_ref[...]).astype(BF16)
    h2 = h1
    n_tiles = w_g.shape[1] // MXU_TILE
    edges = [MXU_TILE * ((n_tiles * c + ffn_chunks - 1) // ffn_chunks) for c in range(ffn_chunks + 1)]
    for lo, hi in zip(edges[:-1], edges[1:]):
        cols = slice(lo, hi)
        g = jnp.dot(u, w_g[:, cols], preferred_element_type=F32)
        up = jnp.dot(u, w_u[:, cols], preferred_element_type=F32)
        act = (g * _sigmoid(g) * up).astype(BF16)
        h2 = h2 + jnp.dot(act, w_d[cols, :], preferred_element_type=F32)
    o_ref[0] = _rms(h2, nfin_ref[...])


def _merge_ffn(x, yr, ya, gr, gm, wts, tm, ffn_chunks):
    b, s, d = x.shape
    const = lambda bi, i: (0, 0)
    wspec = lambda w: pl.BlockSpec(w.shape, const, pipeline_mode=pl.Buffered(1))
    row = pl.BlockSpec((1, tm, d), lambda bi, i: (bi, i, 0))
    return pl.pallas_call(
        functools.partial(_merge_ffn_kernel, ffn_chunks=ffn_chunks),
        grid=(b, s // tm),
        in_specs=[row] * 5 + [wspec(w) for w in wts],
        out_specs=row,
        out_shape=jax.ShapeDtypeStruct((b, s, d), F32),
        compiler_params=pltpu.CompilerParams(
            dimension_semantics=("arbitrary", "arbitrary"), vmem_limit_bytes=VMEM_LIMIT),
        name="merge_ffn",
    )(x, yr, ya, gr, gm, *wts)


def _rope_tables(n_pos):
    pos = np.arange(n_pos, dtype=np.float64)[:, None]
    lane = np.arange(LANES)

    def angles(half):
        inv = ROPE_BASE ** (-np.arange(half, dtype=np.float64) / half)
        return pos * inv[None, :]

    half = RET_DK // 2
    ang = angles(half)[:, lane % half]
    first = (lane % RET_DK) < half
    r_cos = np.cos(ang)
    r_lo = np.where(first, -np.sin(ang), 0.0)
    r_hi = np.where(first, 0.0, np.sin(ang))
    half = MLA_ROPE // 2
    ang = angles(half)[:, lane % half]
    in_rope = (lane >= MLA_NOPE) & (lane < MLA_QK)
    first = in_rope & (lane < MLA_NOPE + half)
    second = in_rope & (lane >= MLA_NOPE + half)
    m_cos = np.where(in_rope, np.cos(ang), 1.0)
    m_lo = np.where(first, -np.sin(ang), 0.0)
    m_hi = np.where(second, np.sin(ang), 0.0)
    return tuple(t.astype(np.float32) for t in (r_cos, r_lo, r_hi, m_cos, m_lo, m_hi))


def _pad_heads(w, heads, width):
    k = w.shape[0]
    w = w.reshape(k, heads, width)
    return jnp.pad(w, ((0, 0), (0, 0), (0, LANES - width))).reshape(k, heads * LANES)


def kernel(x, meta_tokens, norm_mix_w, w_in, ret_decay_fwd, ret_decay_bwd, ret_gn_w, w_ret_out, mla_q_norm_w, w_uq, mla_kv_norm_w, w_uk, w_uv, w_mla_out, w_o, norm_ffn_w, w_ffn_gate, w_ffn_up, w_ffn_down, norm_final_w):
    b, s, d = x.shape
    assert w_in.shape[0] == 1, "one layer: the meta rows are dropped after it"
    ret_qk_w = RET_HEADS * RET_DK
    ret_v_w = RET_HEADS * RET_DV
    q_rank = w_uq.shape[1]
    kv_rank = w_uk.shape[1]
    sizes = (ret_qk_w, ret_qk_w, ret_v_w, ret_v_w, q_rank, kv_rank, MLA_ROPE, d, d)
    assert sum(sizes) == w_in.shape[2]
    bounds = [0]
    for sz in sizes:
        bounds.append(bounds[-1] + sz)
    w_rq, w_rk, w_rv, w_rg, w_cq, w_ckv, w_kr, w_gr, w_gm = (
        w_in[0][:, lo:hi].astype(BF16) for lo, hi in zip(bounds[:-1], bounds[1:]))
    w_lat = jnp.concatenate(
        [w_cq, w_ckv, jnp.pad(w_kr, ((0, 0), (MLA_NOPE, LANES - MLA_QK)))], axis=1)
    row2 = lambda a: a.reshape(1, -1).astype(F32)
    wts1 = (w_rq, w_rk, w_rv, w_rg, w_lat, w_gr, w_gm,
            row2(mla_q_norm_w[0]), _pad_heads(w_uq[0], MLA_HEADS, MLA_QK).astype(BF16),
            row2(mla_kv_norm_w[0]), _pad_heads(w_uk[0], MLA_HEADS, MLA_NOPE).astype(BF16),
            _pad_heads(w_uv[0], MLA_HEADS, MLA_DV).astype(BF16))
    nw = row2(norm_mix_w[0])
    q_scale = (MLA_QK ** -0.5) * math.log2(math.e)

    tabs = _rope_tables(s + N_META)
    tabs_meta = tuple(t[:N_META] for t in tabs)
    tabs_x = tuple(t[N_META:] for t in tabs)
    lgf = -jnp.exp(ret_decay_fwd[0].astype(F32))
    lgb = -jnp.exp(ret_decay_bwd[0].astype(F32))
    rq, rk, rv, sg, q, k, v, gr, gm, sb = _in_proj(
        x, tabs_x, nw, wts1, lgb, q_scale, tm=IN_PROJ_ROWS, with_bwd_states=True)
    meta = meta_tokens.astype(x.dtype)[None]
    _, rk_m, rv_m, _, _, k_m, v_m, _, _ = _in_proj(
        meta, tabs_meta, nw, wts1, lgb, q_scale, tm=N_META, with_bwd_states=False)
    y_att, y_ret = _attention_retention(
        lgf, lgb, q, k, v, k_m[0], v_m[0], rq, rk, rv, sg, sb, rk_m[0], rv_m[0],
        row2(ret_gn_w[0]), tq=ATTN_Q_ROWS, tk=ATTN_KEY_TILE)

    wts4 = (w_ret_out[0].astype(BF16), w_mla_out[0].astype(BF16), w_o[0].astype(BF16),
            row2(norm_ffn_w[0]), w_ffn_gate[0].astype(BF16), w_ffn_up[0].astype(BF16),
            w_ffn_down[0].astype(BF16), row2(norm_final_w))
    return _merge_ffn(x, y_ret, y_att, gr, gm, wts4, tm=MERGE_ROWS, ffn_chunks=FFN_CHUNKS)
```

```python
import functools
import math

import jax
import jax.numpy as jnp
import numpy as np
from jax import lax
from jax.experimental import pallas as pl
from jax.experimental.pallas import tpu as pltpu

N_META = 16
CHUNK = 128
RET_HEADS = 8
RET_DK = 64
RET_DV = 128
MLA_HEADS = 8
MLA_NOPE = 64
MLA_ROPE = 32
MLA_DV = 64
MLA_QK = MLA_NOPE + MLA_ROPE
assert MLA_NOPE == MLA_DV, "k_nope and v share one 64-lanes-per-head up-projection layout"
ROPE_BASE = 10000.0
RMS_EPS = 1e-6
GN_EPS = 1e-5

LANES = 128
MXU_TILE = 256
ONES_LANE = MLA_DV
VMEM_LIMIT = 56 * 1024 * 1024

IN_PROJ_ROWS = 512
ATTN_Q_ROWS = 1024
ATTN_KEY_TILE = 2048
MERGE_ROWS = 512
FFN_CHUNKS = 4

F32 = jnp.float32
BF16 = jnp.bfloat16

_TN = (((0,), (0,)), ((), ()))
_NT = (((1,), (1,)), ((), ()))


def _rms(x, w):
    return x * lax.rsqrt(jnp.mean(x * x, axis=-1, keepdims=True) + RMS_EPS) * w


def _sigmoid(x):
    return 1.0 / (1.0 + jnp.exp(-x))


def _rope_group(x, cos, sin_lo, sin_hi, half):
    return (x * cos + pltpu.roll(x, LANES - half, axis=1) * sin_lo
            + pltpu.roll(x, half, axis=1) * sin_hi)


def _in_proj_kernel(x_ref, nw_ref, w_rq, w_rk, w_rv, w_rg, w_lat, w_gr, w_gm,
                    qnw_ref, w_uq, kvnw_ref, w_ukv,
                    rc_ref, rlo_ref, rhi_ref, mc_ref, mlo_ref, mhi_ref, lgb_ref,
                    rq_o, rk_o, rv_o, sg_o, q_o, k_o, v_o, gr_o, gm_o, *rest, q_scale):
    if rest:
        sb_o, sb_run, wkb, gcb = rest

        @pl.when(pl.program_id(1) == 0)
        def _start_of_sequence():
            for i in range(N_PAIRS):
                wkb[i] = jnp.exp(_pair_scalar(lgb_ref, i, _odd_lane()) * _chunk_pos())
                gcb[i] = jnp.exp(_pair_scalar(lgb_ref, i, _odd_col(8)) * CHUNK)
                sb_run[i] = jnp.zeros((LANES, PAIR_W), F32)

    u = _rms(x_ref[0], nw_ref[...]).astype(BF16)

    def mm(a, w):
        return jnp.dot(a, w[...], preferred_element_type=F32)

    rc, rlo, rhi = rc_ref[...], rlo_ref[...], rhi_ref[...]
    mc, mlo, mhi = mc_ref[...], mlo_ref[...], mhi_ref[...]

    lat = mm(u, w_lat)
    q_rank = qnw_ref.shape[1]
    kv_rank = kvnw_ref.shape[1]
    cq = _rms(lat[:, :q_rank], qnw_ref[...]).astype(BF16)
    ckv = _rms(lat[:, q_rank:q_rank + kv_rank], kvnw_ref[...]).astype(BF16)
    kr = _rope_group(lat[:, q_rank + kv_rank:], mc, mlo, mhi, MLA_ROPE // 2)

    rk = mm(u, w_rk)
    rk_groups = []
    for c in range(rk.shape[1] // LANES):
        sl = slice(c * LANES, (c + 1) * LANES)
        rk_groups.append((_rope_group(rk[:, sl], rc, rlo, rhi, RET_DK // 2)
                          * (RET_DK ** -0.5)).astype(BF16))
        rk_o[0, :, sl] = rk_groups[c]
    rv = mm(u, w_rv).astype(BF16)
    rv_o[0] = rv

    if rest:
        for i in range(N_PAIRS):
            state = sb_run[i]
            for c in range(x_ref.shape[1] // CHUNK - 1, -1, -1):
                rows = slice(c * CHUNK, (c + 1) * CHUNK)
                sb_o[0, c, i] = state.astype(BF16)
                inc = _state_increment(rk_groups[i][rows], wkb[i],
                                       rv[rows, i * PAIR_W:(i + 1) * PAIR_W])
                state = gcb[i][0:1, :] * state + inc
            sb_run[i] = state

    q = mm(cq, w_uq)
    for h in range(MLA_HEADS):
        sl = slice(h * LANES, (h + 1) * LANES)
        q_o[0, :, sl] = (_rope_group(q[:, sl], mc, mlo, mhi, MLA_ROPE // 2) * q_scale).astype(BF16)
    kv = mm(ckv, w_ukv)
    half_w = kv.shape[1] // 2
    lane = lax.broadcasted_iota(jnp.int32, (kv.shape[0], LANES), 1)
    low = lane < MLA_DV
    ones_col = jnp.where(lane == ONES_LANE, 1.0, 0.0)
    for g in range(MLA_HEADS // 2):
        kg = kv[:, g * LANES:(g + 1) * LANES]
        vg = kv[:, half_w + g * LANES:half_w + (g + 1) * LANES]
        pieces = ((kg, vg), (pltpu.roll(kg, MLA_DV, axis=1), pltpu.roll(vg, MLA_DV, axis=1)))
        for j, (kh, vh) in enumerate(pieces):
            sl = slice((2 * g + j) * LANES, (2 * g + j + 1) * LANES)
            k_o[0, :, sl] = jnp.where(low, kh, kr).astype(BF16)
            v_o[0, :, sl] = jnp.where(low, vh, ones_col).astype(BF16)

    rq = mm(u, w_rq)
    for c in range(rq.shape[1] // LANES):
        sl = slice(c * LANES, (c + 1) * LANES)
        rq_o[0, :, sl] = _rope_group(rq[:, sl], rc, rlo, rhi, RET_DK // 2).astype(BF16)

    rg = mm(u, w_rg)
    sg_o[0] = (rg * _sigmoid(rg)).astype(BF16)
    gr_o[0] = _sigmoid(mm(u, w_gr)).astype(BF16)
    gm_o[0] = _sigmoid(mm(u, w_gm)).astype(BF16)


def _in_proj(rows, tabs, nw, wts, lgb, q_scale, tm, with_bwd_states):
    nb, r, d = rows.shape
    n_tiles = r // tm
    const = lambda b, i: (0, 0)

    def wspec(w):
        return pl.BlockSpec(w.shape, const, pipeline_mode=pl.Buffered(1))

    row_spec = lambda width: pl.BlockSpec((1, tm, width), lambda b, i: (b, n_tiles - 1 - i, 0))
    tab_spec = pl.BlockSpec((tm, LANES), lambda b, i: (n_tiles - 1 - i, 0))
    (w_rq, w_rk, w_rv, w_rg, w_lat, w_gr, w_gm, qnw, w_uq, kvnw, w_ukv) = wts
    mla_w = MLA_HEADS * LANES
    out_widths = (w_rq.shape[1], w_rk.shape[1], w_rv.shape[1], w_rg.shape[1],
                  mla_w, mla_w, mla_w, w_gr.shape[1], w_gm.shape[1])
    out_specs = [row_spec(w) for w in out_widths]
    out_shape = [jax.ShapeDtypeStruct((nb, r, w), BF16) for w in out_widths]
    scratch = []
    if with_bwd_states:
        cpb = tm // CHUNK
        out_specs.append(pl.BlockSpec((1, cpb, N_PAIRS, LANES, PAIR_W),
                                      lambda b, i: (b, n_tiles - 1 - i, 0, 0, 0)))
        out_shape.append(jax.ShapeDtypeStruct((nb, r // CHUNK, N_PAIRS, LANES, PAIR_W), BF16))
        scratch = [
            pltpu.VMEM((N_PAIRS, LANES, PAIR_W), F32),
            pltpu.VMEM((N_PAIRS, CHUNK, LANES), F32),
            pltpu.VMEM((N_PAIRS, 8, PAIR_W), F32),
        ]
    return pl.pallas_call(
        functools.partial(_in_proj_kernel, q_scale=q_scale),
        grid=(nb, n_tiles),
        in_specs=([row_spec(d), wspec(nw)] + [wspec(w) for w in wts] + [tab_spec] * 6
                  + [pl.BlockSpec(memory_space=pltpu.SMEM)]),
        out_specs=out_specs,
        out_shape=out_shape,
        scratch_shapes=scratch,
        compiler_params=pltpu.CompilerParams(
            dimension_semantics=("arbitrary", "arbitrary"), vmem_limit_bytes=VMEM_LIMIT),
        name="in_proj",
    )(rows, nw, *wts, *tabs, lgb)


N_PAIRS = RET_HEADS // 2
PAIR_W = 2 * RET_DV


def _odd_lane():
    return lax.broadcasted_iota(jnp.int32, (CHUNK, LANES), 1) >= RET_DK


def _odd_col(rows=CHUNK):
    return lax.broadcasted_iota(jnp.int32, (rows, PAIR_W), 1) >= RET_DV


def _own_head():
    row = lax.broadcasted_iota(jnp.int32, (LANES, PAIR_W), 0)
    return (row >= RET_DK) == _odd_col(LANES)


def _chunk_pos():
    return lax.broadcasted_iota(jnp.int32, (CHUNK, LANES), 0).astype(F32)


def _pair_scalar(ref, i, odd):
    return jnp.where(odd, ref[2 * i + 1], ref[2 * i])


def _state_increment(kp, w, vp):
    kw = (kp.astype(F32) * w).astype(BF16)
    kv = lax.dot_general(kw, vp, _TN, preferred_element_type=F32)
    return jnp.where(_own_head(), kv, 0.0)


def _retention_tables(lgf_ref, lgb_ref, dmask, wqf, wqb, wkf, gcf):
    pos = _chunk_pos()
    row2 = lax.broadcasted_iota(jnp.int32, (CHUNK, PAIR_W), 0)
    col2 = lax.broadcasted_iota(jnp.int32, (CHUNK, PAIR_W), 1)
    odd_col = _odd_col()
    rel = (row2 - jnp.where(odd_col, col2 - RET_DV, col2)).astype(F32)
    for i in range(N_PAIRS):
        lf = _pair_scalar(lgf_ref, i, _odd_lane())
        lb = _pair_scalar(lgb_ref, i, _odd_lane())
        lf2 = _pair_scalar(lgf_ref, i, odd_col)
        lb2 = _pair_scalar(lgb_ref, i, odd_col)
        dmask[i] = jnp.exp(jnp.where(rel >= 0, lf2 * rel, -lb2 * rel))
        wqf[i] = jnp.exp(lf * (pos + 1.0))
        wqb[i] = jnp.exp(lb * (CHUNK - pos))
        wkf[i] = jnp.exp(lf * (CHUNK - 1.0 - pos))
        gcf[i] = jnp.exp(_pair_scalar(lgf_ref, i, _odd_col(8)) * CHUNK)


def _retention_meta_state(lgf_ref, km_ref, vm_ref, sf_run):
    mrow = lax.broadcasted_iota(jnp.int32, (N_META, LANES), 0).astype(F32)
    odd_m = lax.broadcasted_iota(jnp.int32, (N_META, LANES), 1) >= RET_DK
    for i in range(N_PAIRS):
        wm = jnp.exp(_pair_scalar(lgf_ref, i, odd_m) * (N_META - 1.0 - mrow))
        kmw = (km_ref[:, i * LANES:(i + 1) * LANES].astype(F32) * wm).astype(BF16)
        kv = lax.dot_general(kmw, vm_ref[:, i * PAIR_W:(i + 1) * PAIR_W], _TN,
                             preferred_element_type=F32)
        sf_run[i] = jnp.where(_own_head(), kv, 0.0)


def _retention_chunk(q_ref, k_ref, v_ref, sg_ref, sb_ref, gnw_ref, o_ref,
                     sf_run, dmask, wqf, wqb, wkf, gcf):
    zeros = jnp.zeros((CHUNK, RET_DV), BF16)
    odd_lane = _odd_lane()
    for i in range(N_PAIRS):
        qb16 = q_ref[0, :, i * LANES:(i + 1) * LANES]
        kp = k_ref[0, :, i * LANES:(i + 1) * LANES]
        vp = v_ref[0, :, i * PAIR_W:(i + 1) * PAIR_W]
        qp = qb16.astype(F32)
        kpf = kp.astype(F32)
        k2 = jnp.concatenate([jnp.where(odd_lane, 0.0, kpf).astype(BF16),
                              jnp.where(odd_lane, kpf, 0.0).astype(BF16)], axis=0)
        s2 = lax.dot_general(qb16, k2, _NT, preferred_element_type=F32)
        v_diag = jnp.concatenate([
            jnp.concatenate([vp[:, :RET_DV], zeros], axis=1),
            jnp.concatenate([zeros, vp[:, RET_DV:]], axis=1)], axis=0)
        q_cross = jnp.concatenate([(qp * wqf[i]).astype(BF16),
                                   (qp * wqb[i]).astype(BF16)], axis=1)
        state = sf_run[i]
        states = jnp.concatenate([state.astype(BF16), sb_ref[0, 0, i]], axis=0)
        o2 = (jnp.dot((s2 * dmask[i]).astype(BF16), v_diag, preferred_element_type=F32)
              + jnp.dot(q_cross, states, preferred_element_type=F32))
        sf_run[i] = gcf[i][0:1, :] * state + _state_increment(kp, wkf[i], vp)
        for j in range(2):
            o = o2[:, j * RET_DV:(j + 1) * RET_DV]
            mu = jnp.mean(o, axis=-1, keepdims=True)
            d = o - mu
            var = jnp.mean(d * d, axis=-1, keepdims=True)
            yn = d * lax.rsqrt(var + GN_EPS)
            hs = slice((2 * i + j) * RET_DV, (2 * i + j + 1) * RET_DV)
            y = yn * gnw_ref[:, hs] * sg_ref[0, :, hs].astype(F32)
            o_ref[0, :, hs] = y.astype(BF16)


def _attn_ret_kernel(lgf_ref, lgb_ref, q_ref, k_ref, v_ref, km_ref, vm_ref,
                     rq_ref, rk_ref, rv_ref, sg_ref, sb_ref, rkm_ref, rvm_ref, gnw_ref,
                     o_ref, yr_ref, sf_run, dmask, wqf, wqb, wkf, gcf, *, tk):
    @pl.when((pl.program_id(1) == 0) & (pl.program_id(2) == 0))
    def _start_of_sequence():
        _retention_tables(lgf_ref, lgb_ref, dmask, wqf, wqb, wkf, gcf)
        _retention_meta_state(lgf_ref, rkm_ref, rvm_ref, sf_run)

    _retention_chunk(rq_ref, rk_ref, rv_ref, sg_ref, sb_ref, gnw_ref, yr_ref,
                     sf_run, dmask, wqf, wqb, wkf, gcf)

    q = q_ref[0]
    s0 = lax.dot_general(q, km_ref[...], _NT, preferred_element_type=F32)
    m0 = jnp.max(s0, axis=1, keepdims=True)
    acc0 = jnp.dot(jnp.exp2(s0 - m0).astype(BF16), vm_ref[...], preferred_element_type=F32)

    m, acc = m0, acc0
    for j in range(k_ref.shape[1] // tk):
        kt = k_ref[0, j * tk:(j + 1) * tk, :]
        vt = v_ref[0, j * tk:(j + 1) * tk, :]
        s = lax.dot_general(q, kt, _NT, preferred_element_type=F32)
        m_new = jnp.maximum(m, jnp.max(s, axis=1, keepdims=True))
        p = jnp.exp2(s - m_new).astype(BF16)
        acc = jnp.exp2(m - m_new) * acc + jnp.dot(p, vt, preferred_element_type=F32)
        m = m_new
    o_ref[0] = (acc / acc[:, ONES_LANE:ONES_LANE + 1]).astype(BF16)


def _attention_retention(lgf, lgb, q, k, v, km, vm, rq, rk, rv, sg, sb, rkm, rvm, gnw, tq, tk):
    b, s, w = q.shape
    heads = w // LANES
    n_q = s // tq
    assert heads * n_q == s // CHUNK, "one retention chunk per attention grid step"
    v_w = rv.shape[2]
    att = lambda bi, h, i, *_: (bi, i, h)
    kv = lambda bi, h, i, *_: (bi, 0, h)
    meta = lambda bi, h, i, *_: (0, h)
    chunk = lambda bi, h, i, *_: (bi, h * n_q + i, 0)
    const = lambda bi, h, i, *_: (0, 0)
    tab = lambda rows, width: pltpu.VMEM((N_PAIRS, rows, width), F32)
    grid_spec = pltpu.PrefetchScalarGridSpec(
        num_scalar_prefetch=2,
        grid=(b, heads, n_q),
        in_specs=[
            pl.BlockSpec((1, tq, LANES), att),
            pl.BlockSpec((1, s, LANES), kv),
            pl.BlockSpec((1, s, LANES), kv),
            pl.BlockSpec((N_META, LANES), meta),
            pl.BlockSpec((N_META, LANES), meta),
            pl.BlockSpec((1, CHUNK, rq.shape[2]), chunk),
            pl.BlockSpec((1, CHUNK, rk.shape[2]), chunk),
            pl.BlockSpec((1, CHUNK, v_w), chunk),
            pl.BlockSpec((1, CHUNK, v_w), chunk),
            pl.BlockSpec((1, 1, N_PAIRS, LANES, PAIR_W),
                         lambda bi, h, i, *_: (bi, h * n_q + i, 0, 0, 0)),
            pl.BlockSpec(rkm.shape, const),
            pl.BlockSpec(rvm.shape, const),
            pl.BlockSpec(gnw.shape, const),
        ],
        out_specs=[pl.BlockSpec((1, tq, LANES), att), pl.BlockSpec((1, CHUNK, v_w), chunk)],
        scratch_shapes=[
            tab(LANES, PAIR_W),
            tab(CHUNK, PAIR_W),
            tab(CHUNK, LANES), tab(CHUNK, LANES), tab(CHUNK, LANES),
            tab(8, PAIR_W),
        ],
    )
    return pl.pallas_call(
        functools.partial(_attn_ret_kernel, tk=tk),
        grid_spec=grid_spec,
        out_shape=[jax.ShapeDtypeStruct((b, s, w), BF16), jax.ShapeDtypeStruct((b, s, v_w), BF16)],
        compiler_params=pltpu.CompilerParams(
            dimension_semantics=("arbitrary", "arbitrary", "arbitrary"),
            vmem_limit_bytes=VMEM_LIMIT),
        name="attention_retention",
    )(lgf, lgb, q, k, v, km, vm, rq, rk, rv, sg, sb, rkm, rvm, gnw)


def _merge_ffn_kernel(x_ref, yr_ref, ya_ref, gr_ref, gm_ref, w_ro, w_mo, w_o, nfw_ref,
                      w_g, w_u, w_d, nfin_ref, o_ref, *, ffn_chunks):
    def mm(a, w):
        return jnp.dot(a, w[...], preferred_element_type=F32)

    lane = lax.broadcasted_iota(jnp.int32, (ya_ref.shape[1], LANES), 1)
    packed = []
    for i in range(MLA_HEADS // 2):
        even = ya_ref[0, :, (2 * i) * LANES:(2 * i + 1) * LANES].astype(F32)
        odd = ya_ref[0, :, (2 * i + 1) * LANES:(2 * i + 2) * LANES].astype(F32)
        packed.append(jnp.where(lane < MLA_DV, even, pltpu.roll(odd, MLA_DV, axis=1)))
    ya = jnp.concatenate(packed, axis=1).astype(BF16)
    merged = (gr_ref[0].astype(F32) * mm(yr_ref[0], w_ro)
              + gm_ref[0].astype(F32) * mm(ya, w_mo))
    h1 = x_ref[0] + mm(merged.astype(BF16), w_o)
    u = _rms(h1, nfw_ref[...]).astype(BF16)
    h2 = h1
    n_tiles = w_g.shape[1] // MXU_TILE
    edges = [MXU_TILE * ((n_tiles * c + ffn_chunks - 1) // ffn_chunks) for c in range(ffn_chunks + 1)]
    for lo, hi in zip(edges[:-1], edges[1:]):
        cols = slice(lo, hi)
        g = jnp.dot(u, w_g[:, cols], preferred_element_type=F32)
        up = jnp.dot(u, w_u[:, cols], preferred_element_type=F32)
        act = (g * _sigmoid(g) * up).astype(BF16)
        h2 = h2 + jnp.dot(act, w_d[cols, :], preferred_element_type=F32)
    o_ref[0] = _rms(h2, nfin_ref[...])


def _merge_ffn(x, yr, ya, gr, gm, wts, tm, ffn_chunks):
    b, s, d = x.shape
    const = lambda bi, i: (0, 0)
    wspec = lambda w: pl.BlockSpec(w.shape, const, pipeline_mode=pl.Buffered(1))
    row = pl.BlockSpec((1, tm, d), lambda bi, i: (bi, i, 0))
    return pl.pallas_call(
        functools.partial(_merge_ffn_kernel, ffn_chunks=ffn_chunks),
        grid=(b, s // tm),
        in_specs=[row] * 5 + [wspec(w) for w in wts],
        out_specs=row,
        out_shape=jax.ShapeDtypeStruct((b, s, d), F32),
        compiler_params=pltpu.CompilerParams(
            dimension_semantics=("arbitrary", "arbitrary"), vmem_limit_bytes=VMEM_LIMIT),
        name="merge_ffn",
    )(x, yr, ya, gr, gm, *wts)


def _rope_tables(n_pos):
    pos = np.arange(n_pos, dtype=np.float64)[:, None]
    lane = np.arange(LANES)

    def angles(half):
        inv = ROPE_BASE ** (-np.arange(half, dtype=np.float64) / half)
        return pos * inv[None, :]

    half = RET_DK // 2
    ang = angles(half)[:, lane % half]
    first = (lane % RET_DK) < half
    r_cos = np.cos(ang)
    r_lo = np.where(first, -np.sin(ang), 0.0)
    r_hi = np.where(first, 0.0, np.sin(ang))
    half = MLA_ROPE // 2
    ang = angles(half)[:, lane % half]
    in_rope = (lane >= MLA_NOPE) & (lane < MLA_QK)
    first = in_rope & (lane < MLA_NOPE + half)
    second = in_rope & (lane >= MLA_NOPE + half)
    m_cos = np.where(in_rope, np.cos(ang), 1.0)
    m_lo = np.where(first, -np.sin(ang), 0.0)
    m_hi = np.where(second, np.sin(ang), 0.0)
    return tuple(t.astype(np.float32) for t in (r_cos, r_lo, r_hi, m_cos, m_lo, m_hi))


def _pad_heads(w, heads, width):
    k = w.shape[0]
    w = w.reshape(k, heads, width)
    return jnp.pad(w, ((0, 0), (0, 0), (0, LANES - width))).reshape(k, heads * LANES)


def kernel(x, meta_tokens, norm_mix_w, w_in, ret_decay_fwd, ret_decay_bwd, ret_gn_w, w_ret_out, mla_q_norm_w, w_uq, mla_kv_norm_w, w_uk, w_uv, w_mla_out, w_o, norm_ffn_w, w_ffn_gate, w_ffn_up, w_ffn_down, norm_final_w):
    b, s, d = x.shape
    assert w_in.shape[0] == 1, "one layer: the meta rows are dropped after it"
    ret_qk_w = RET_HEADS * RET_DK
    ret_v_w = RET_HEADS * RET_DV
    q_rank = w_uq.shape[1]
    kv_rank = w_uk.shape[1]
    sizes = (ret_qk_w, ret_qk_w, ret_v_w, ret_v_w, q_rank, kv_rank, MLA_ROPE, d, d)
    assert sum(sizes) == w_in.shape[2]
    bounds = [0]
    for sz in sizes:
        bounds.append(bounds[-1] + sz)
    w_rq, w_rk, w_rv, w_rg, w_cq, w_ckv, w_kr, w_gr, w_gm = (
        w_in[0][:, lo:hi].astype(BF16) for lo, hi in zip(bounds[:-1], bounds[1:]))
    w_lat = jnp.concatenate(
        [w_cq, w_ckv, jnp.pad(w_kr, ((0, 0), (MLA_NOPE, LANES - MLA_QK)))], axis=1)
    row2 = lambda a: a.reshape(1, -1).astype(F32)
    wts1 = (w_rq, w_rk, w_rv, w_rg, w_lat, w_gr, w_gm,
            row2(mla_q_norm_w[0]), _pad_heads(w_uq[0], MLA_HEADS, MLA_QK).astype(BF16),
            row2(mla_kv_norm_w[0]),
            jnp.concatenate([w_uk[0], w_uv[0]], axis=1).astype(BF16))
    nw = row2(norm_mix_w[0])
    q_scale = (MLA_QK ** -0.5) * math.log2(math.e)

    tabs = _rope_tables(s + N_META)
    tabs_meta = tuple(t[:N_META] for t in tabs)
    tabs_x = tuple(t[N_META:] for t in tabs)
    lgf = -jnp.exp(ret_decay_fwd[0].astype(F32))
    lgb = -jnp.exp(ret_decay_bwd[0].astype(F32))
    rq, rk, rv, sg, q, k, v, gr, gm, sb = _in_proj(
        x, tabs_x, nw, wts1, lgb, q_scale, tm=IN_PROJ_ROWS, with_bwd_states=True)
    meta = meta_tokens.astype(x.dtype)[None]
    _, rk_m, rv_m, _, _, k_m, v_m, _, _ = _in_proj(
        meta, tabs_meta, nw, wts1, lgb, q_scale, tm=N_META, with_bwd_states=False)
    y_att, y_ret = _attention_retention(
        lgf, lgb, q, k, v, k_m[0], v_m[0], rq, rk, rv, sg, sb, rk_m[0], rv_m[0],
        row2(ret_gn_w[0]), tq=ATTN_Q_ROWS, tk=ATTN_KEY_TILE)

    wts4 = (w_ret_out[0].astype(BF16), w_mla_out[0].astype(BF16), w_o[0].astype(BF16),
            row2(norm_ffn_w[0]), w_ffn_gate[0].astype(BF16), w_ffn_up[0].astype(BF16),
            w_ffn_down[0].astype(BF16), row2(norm_final_w))
    return _merge_ffn(x, y_ret, y_att, gr, gm, wts4, tm=MERGE_ROWS, ffn_chunks=FFN_CHUNKS)
```

```python
import functools
import math

import jax
import jax.numpy as jnp
import numpy as np
from jax import lax
from jax.experimental import pallas as pl
from jax.experimental.pallas import tpu as pltpu

N_META = 16
CHUNK = 128
RET_HEADS = 8
RET_DK = 64
RET_DV = 128
MLA_HEADS = 8
MLA_NOPE = 64
MLA_ROPE = 32
MLA_DV = 64
MLA_QK = MLA_NOPE + MLA_ROPE
assert MLA_NOPE == MLA_DV, "k_nope and v share one 64-lanes-per-head up-projection layout"
ROPE_BASE = 10000.0
RMS_EPS = 1e-6
GN_EPS = 1e-5

LANES = 128
MXU_TILE = 256
ONES_LANE = MLA_DV
VMEM_LIMIT = 56 * 1024 * 1024

IN_PROJ_ROWS = 512
ATTN_Q_ROWS = 1024
ATTN_KEY_TILE = 2048
MERGE_ROWS = 512
FFN_CHUNKS = 4

F32 = jnp.float32
BF16 = jnp.bfloat16

_TN = (((0,), (0,)), ((), ()))
_NT = (((1,), (1,)), ((), ()))


def _rms(x, w):
    return x * lax.rsqrt(jnp.mean(x * x, axis=-1, keepdims=True) + RMS_EPS) * w


def _sigmoid(x):
    return 1.0 / (1.0 + jnp.exp(-x))


def _rope_group(x, cos, sin_lo, sin_hi, half):
    return (x * cos + pltpu.roll(x, LANES - half, axis=1) * sin_lo
            + pltpu.roll(x, half, axis=1) * sin_hi)


def _in_proj_kernel(x_ref, nw_ref, w_rq, w_rk, w_rv, w_rg, w_lat, w_gr, w_gm,
                    qnw_ref, w_uq, kvnw_ref, w_ukv,
                    rc_ref, rlo_ref, rhi_ref, mc_ref, mlo_ref, mhi_ref, lgb_ref,
                    rq_o, rk_o, rv_o, sg_o, q_o, k_o, v_o, gr_o, gm_o, *rest, q_scale):
    if rest:
        sb_o, sb_run, wkb, gcb = rest

        @pl.when(pl.program_id(1) == 0)
        def _start_of_sequence():
            for i in range(N_PAIRS):
                wkb[i] = jnp.exp(_pair_scalar(lgb_ref, i, _odd_lane()) * _chunk_pos())
                gcb[i] = jnp.exp(_pair_scalar(lgb_ref, i, _odd_col(8)) * CHUNK)
                sb_run[i] = jnp.zeros((LANES, PAIR_W), F32)

    x = x_ref[0]
    u = (x * nw_ref[...]).astype(BF16)
    inv_rms = lax.rsqrt(jnp.mean(x * x, axis=-1, keepdims=True) + RMS_EPS)

    def mm(a, w):
        return jnp.dot(a, w[...], preferred_element_type=F32)

    def mm_u(w):
        return mm(u, w) * inv_rms

    rc, rlo, rhi = rc_ref[...], rlo_ref[...], rhi_ref[...]
    mc, mlo, mhi = mc_ref[...], mlo_ref[...], mhi_ref[...]

    lat = mm_u(w_lat)
    q_rank = qnw_ref.shape[1]
    kv_rank = kvnw_ref.shape[1]
    cq = _rms(lat[:, :q_rank], qnw_ref[...]).astype(BF16)
    ckv = _rms(lat[:, q_rank:q_rank + kv_rank], kvnw_ref[...]).astype(BF16)
    kr = _rope_group(lat[:, q_rank + kv_rank:], mc, mlo, mhi, MLA_ROPE // 2)

    rk = mm_u(w_rk)
    rk_groups = []
    for c in range(rk.shape[1] // LANES):
        sl = slice(c * LANES, (c + 1) * LANES)
        rk_groups.append((_rope_group(rk[:, sl], rc, rlo, rhi, RET_DK // 2)
                          * (RET_DK ** -0.5)).astype(BF16))
        rk_o[0, :, sl] = rk_groups[c]
    rv = mm_u(w_rv).astype(BF16)
    rv_o[0] = rv

    if rest:
        for i in range(N_PAIRS):
            state = sb_run[i]
            for c in range(x_ref.shape[1] // CHUNK - 1, -1, -1):
                rows = slice(c * CHUNK, (c + 1) * CHUNK)
                sb_o[0, c, i] = state.astype(BF16)
                inc = _state_increment(rk_groups[i][rows], wkb[i],
                                       rv[rows, i * PAIR_W:(i + 1) * PAIR_W])
                state = gcb[i][0:1, :] * state + inc
            sb_run[i] = state

    q = mm(cq, w_uq)
    for h in range(MLA_HEADS):
        sl = slice(h * LANES, (h + 1) * LANES)
        q_o[0, :, sl] = (_rope_group(q[:, sl], mc, mlo, mhi, MLA_ROPE // 2) * q_scale).astype(BF16)
    kv = mm(ckv, w_ukv)
    half_w = kv.shape[1] // 2
    lane = lax.broadcasted_iota(jnp.int32, (kv.shape[0], LANES), 1)
    low = lane < MLA_DV
    ones_col = jnp.where(lane == ONES_LANE, 1.0, 0.0)
    for g in range(MLA_HEADS // 2):
        kg = kv[:, g * LANES:(g + 1) * LANES]
        vg = kv[:, half_w + g * LANES:half_w + (g + 1) * LANES]
        pieces = ((kg, vg), (pltpu.roll(kg, MLA_DV, axis=1), pltpu.roll(vg, MLA_DV, axis=1)))
        for j, (kh, vh) in enumerate(pieces):
            sl = slice((2 * g + j) * LANES, (2 * g + j + 1) * LANES)
            k_o[0, :, sl] = jnp.where(low, kh, kr).astype(BF16)
            v_o[0, :, sl] = jnp.where(low, vh, ones_col).astype(BF16)

    rq = mm_u(w_rq)
    for c in range(rq.shape[1] // LANES):
        sl = slice(c * LANES, (c + 1) * LANES)
        rq_o[0, :, sl] = _rope_group(rq[:, sl], rc, rlo, rhi, RET_DK // 2).astype(BF16)

    rg = mm_u(w_rg)
    sg_o[0] = (rg * _sigmoid(rg)).astype(BF16)
    gr_o[0] = _sigmoid(mm_u(w_gr)).astype(BF16)
    gm_o[0] = _sigmoid(mm_u(w_gm)).astype(BF16)


def _in_proj(rows, tabs, nw, wts, lgb, q_scale, tm, with_bwd_states):
    nb, r, d = rows.shape
    n_tiles = r // tm
    const = lambda b, i: (0, 0)

    def wspec(w):
        return pl.BlockSpec(w.shape, const, pipeline_mode=pl.Buffered(1))

    row_spec = lambda width: pl.BlockSpec((1, tm, width), lambda b, i: (b, n_tiles - 1 - i, 0))
    tab_spec = pl.BlockSpec((tm, LANES), lambda b, i: (n_tiles - 1 - i, 0))
    (w_rq, w_rk, w_rv, w_rg, w_lat, w_gr, w_gm, qnw, w_uq, kvnw, w_ukv) = wts
    mla_w = MLA_HEADS * LANES
    out_widths = (w_rq.shape[1], w_rk.shape[1], w_rv.shape[1], w_rg.shape[1],
                  mla_w, mla_w, mla_w, w_gr.shape[1], w_gm.shape[1])
    out_specs = [row_spec(w) for w in out_widths]
    out_shape = [jax.ShapeDtypeStruct((nb, r, w), BF16) for w in out_widths]
    scratch = []
    if with_bwd_states:
        cpb = tm // CHUNK
        out_specs.append(pl.BlockSpec((1, cpb, N_PAIRS, LANES, PAIR_W),
                                      lambda b, i: (b, n_tiles - 1 - i, 0, 0, 0)))
        out_shape.append(jax.ShapeDtypeStruct((nb, r // CHUNK, N_PAIRS, LANES, PAIR_W), BF16))
        scratch = [
            pltpu.VMEM((N_PAIRS, LANES, PAIR_W), F32),
            pltpu.VMEM((N_PAIRS, CHUNK, LANES), F32),
            pltpu.VMEM((N_PAIRS, 8, PAIR_W), F32),
        ]
    return pl.pallas_call(
        functools.partial(_in_proj_kernel, q_scale=q_scale),
        grid=(nb, n_tiles),
        in_specs=([row_spec(d), wspec(nw)] + [wspec(w) for w in wts] + [tab_spec] * 6
                  + [pl.BlockSpec(memory_space=pltpu.SMEM)]),
        out_specs=out_specs,
        out_shape=out_shape,
        scratch_shapes=scratch,
        compiler_params=pltpu.CompilerParams(
            dimension_semantics=("arbitrary", "arbitrary"), vmem_limit_bytes=VMEM_LIMIT),
        name="in_proj",
    )(rows, nw, *wts, *tabs, lgb)


N_PAIRS = RET_HEADS // 2
PAIR_W = 2 * RET_DV


def _odd_lane():
    return lax.broadcasted_iota(jnp.int32, (CHUNK, LANES), 1) >= RET_DK


def _odd_col(rows=CHUNK):
    return lax.broadcasted_iota(jnp.int32, (rows, PAIR_W), 1) >= RET_DV


def _own_head():
    row = lax.broadcasted_iota(jnp.int32, (LANES, PAIR_W), 0)
    return (row >= RET_DK) == _odd_col(LANES)


def _chunk_pos():
    return lax.broadcasted_iota(jnp.int32, (CHUNK, LANES), 0).astype(F32)


def _pair_scalar(ref, i, odd):
    return jnp.where(odd, ref[2 * i + 1], ref[2 * i])


def _state_increment(kp, w, vp):
    kw = (kp.astype(F32) * w).astype(BF16)
    kv = lax.dot_general(kw, vp, _TN, preferred_element_type=F32)
    return jnp.where(_own_head(), kv, 0.0)


def _retention_tables(lgf_ref, lgb_ref, dmask, wqf, wqb, wkf, gcf):
    pos = _chunk_pos()
    row2 = lax.broadcasted_iota(jnp.int32, (CHUNK, PAIR_W), 0)
    col2 = lax.broadcasted_iota(jnp.int32, (CHUNK, PAIR_W), 1)
    odd_col = _odd_col()
    rel = (row2 - jnp.where(odd_col, col2 - RET_DV, col2)).astype(F32)
    for i in range(N_PAIRS):
        lf = _pair_scalar(lgf_ref, i, _odd_lane())
        lb = _pair_scalar(lgb_ref, i, _odd_lane())
        lf2 = _pair_scalar(lgf_ref, i, odd_col)
        lb2 = _pair_scalar(lgb_ref, i, odd_col)
        dmask[i] = jnp.exp(jnp.where(rel >= 0, lf2 * rel, -lb2 * rel))
        wqf[i] = jnp.exp(lf * (pos + 1.0))
        wqb[i] = jnp.exp(lb * (CHUNK - pos))
        wkf[i] = jnp.exp(lf * (CHUNK - 1.0 - pos))
        gcf[i] = jnp.exp(_pair_scalar(lgf_ref, i, _odd_col(8)) * CHUNK)


def _retention_meta_state(lgf_ref, km_ref, vm_ref, sf_run):
    mrow = lax.broadcasted_iota(jnp.int32, (N_META, LANES), 0).astype(F32)
    odd_m = lax.broadcasted_iota(jnp.int32, (N_META, LANES), 1) >= RET_DK
    for i in range(N_PAIRS):
        wm = jnp.exp(_pair_scalar(lgf_ref, i, odd_m) * (N_META - 1.0 - mrow))
        kmw = (km_ref[:, i * LANES:(i + 1) * LANES].astype(F32) * wm).astype(BF16)
        kv = lax.dot_general(kmw, vm_ref[:, i * PAIR_W:(i + 1) * PAIR_W], _TN,
                             preferred_element_type=F32)
        sf_run[i] = jnp.where(_own_head(), kv, 0.0)


def _retention_chunk(q_ref, k_ref, v_ref, sg_ref, sb_ref, gnw_ref, o_ref,
                     sf_run, dmask, wqf, wqb, wkf, gcf):
    zeros = jnp.zeros((CHUNK, RET_DV), BF16)
    odd_lane = _odd_lane()
    for i in range(N_PAIRS):
        qb16 = q_ref[0, :, i * LANES:(i + 1) * LANES]
        kp = k_ref[0, :, i * LANES:(i + 1) * LANES]
        vp = v_ref[0, :, i * PAIR_W:(i + 1) * PAIR_W]
        qp = qb16.astype(F32)
        kpf = kp.astype(F32)
        k2 = jnp.concatenate([jnp.where(odd_lane, 0.0, kpf).astype(BF16),
                              jnp.where(odd_lane, kpf, 0.0).astype(BF16)], axis=0)
        s2 = lax.dot_general(qb16, k2, _NT, preferred_element_type=F32)
        v_diag = jnp.concatenate([
            jnp.concatenate([vp[:, :RET_DV], zeros], axis=1),
            jnp.concatenate([zeros, vp[:, RET_DV:]], axis=1)], axis=0)
        q_cross = jnp.concatenate([(qp * wqf[i]).astype(BF16),
                                   (qp * wqb[i]).astype(BF16)], axis=1)
        state = sf_run[i]
        states = jnp.concatenate([state.astype(BF16), sb_ref[0, 0, i]], axis=0)
        o2 = (jnp.dot((s2 * dmask[i]).astype(BF16), v_diag, preferred_element_type=F32)
              + jnp.dot(q_cross, states, preferred_element_type=F32))
        sf_run[i] = gcf[i][0:1, :] * state + _state_increment(kp, wkf[i], vp)
        for j in range(2):
            o = o2[:, j * RET_DV:(j + 1) * RET_DV]
            mu = jnp.mean(o, axis=-1, keepdims=True)
            d = o - mu
            var = jnp.mean(d * d, axis=-1, keepdims=True)
            yn = d * lax.rsqrt(var + GN_EPS)
            hs = slice((2 * i + j) * RET_DV, (2 * i + j + 1) * RET_DV)
            y = yn * gnw_ref[:, hs] * sg_ref[0, :, hs].astype(F32)
            o_ref[0, :, hs] = y.astype(BF16)


def _attn_ret_kernel(lgf_ref, lgb_ref, q_ref, k_ref, v_ref, km_ref, vm_ref,
                     rq_ref, rk_ref, rv_ref, sg_ref, sb_ref, rkm_ref, rvm_ref, gnw_ref,
                     o_ref, yr_ref, sf_run, dmask, wqf, wqb, wkf, gcf, *, tk):
    @pl.when((pl.program_id(1) == 0) & (pl.program_id(2) == 0))
    def _start_of_sequence():
        _retention_tables(lgf_ref, lgb_ref, dmask, wqf, wqb, wkf, gcf)
        _retention_meta_state(lgf_ref, rkm_ref, rvm_ref, sf_run)

    _retention_chunk(rq_ref, rk_ref, rv_ref, sg_ref, sb_ref, gnw_ref, yr_ref,
                     sf_run, dmask, wqf, wqb, wkf, gcf)

    q = q_ref[0]
    s0 = lax.dot_general(q, km_ref[...], _NT, preferred_element_type=F32)
    m0 = jnp.max(s0, axis=1, keepdims=True)
    acc0 = jnp.dot(jnp.exp2(s0 - m0).astype(BF16), vm_ref[...], preferred_element_type=F32)

    m, acc = m0, acc0
    for j in range(k_ref.shape[1] // tk):
        kt = k_ref[0, j * tk:(j + 1) * tk, :]
        vt = v_ref[0, j * tk:(j + 1) * tk, :]
        s = lax.dot_general(q, kt, _NT, preferred_element_type=F32)
        m_new = jnp.maximum(m, jnp.max(s, axis=1, keepdims=True))
        p = jnp.exp2(s - m_new).astype(BF16)
        acc = jnp.exp2(m - m_new) * acc + jnp.dot(p, vt, preferred_element_type=F32)
        m = m_new
    o_ref[0] = (acc / acc[:, ONES_LANE:ONES_LANE + 1]).astype(BF16)


def _attention_retention(lgf, lgb, q, k, v, km, vm, rq, rk, rv, sg, sb, rkm, rvm, gnw, tq, tk):
    b, s, w = q.shape
    heads = w // LANES
    n_q = s // tq
    assert heads * n_q == s // CHUNK, "one retention chunk per attention grid step"
    v_w = rv.shape[2]
    att = lambda bi, h, i, *_: (bi, i, h)
    kv = lambda bi, h, i, *_: (bi, 0, h)
    meta = lambda bi, h, i, *_: (0, h)
    chunk = lambda bi, h, i, *_: (bi, h * n_q + i, 0)
    const = lambda bi, h, i, *_: (0, 0)
    tab = lambda rows, width: pltpu.VMEM((N_PAIRS, rows, width), F32)
    grid_spec = pltpu.PrefetchScalarGridSpec(
        num_scalar_prefetch=2,
        grid=(b, heads, n_q),
        in_specs=[
            pl.BlockSpec((1, tq, LANES), att),
            pl.BlockSpec((1, s, LANES), kv),
            pl.BlockSpec((1, s, LANES), kv),
            pl.BlockSpec((N_META, LANES), meta),
            pl.BlockSpec((N_META, LANES), meta),
            pl.BlockSpec((1, CHUNK, rq.shape[2]), chunk),
            pl.BlockSpec((1, CHUNK, rk.shape[2]), chunk),
            pl.BlockSpec((1, CHUNK, v_w), chunk),
            pl.BlockSpec((1, CHUNK, v_w), chunk),
            pl.BlockSpec((1, 1, N_PAIRS, LANES, PAIR_W),
                         lambda bi, h, i, *_: (bi, h * n_q + i, 0, 0, 0)),
            pl.BlockSpec(rkm.shape, const),
            pl.BlockSpec(rvm.shape, const),
            pl.BlockSpec(gnw.shape, const),
        ],
        out_specs=[pl.BlockSpec((1, tq, LANES), att), pl.BlockSpec((1, CHUNK, v_w), chunk)],
        scratch_shapes=[
            tab(LANES, PAIR_W),
            tab(CHUNK, PAIR_W),
            tab(CHUNK, LANES), tab(CHUNK, LANES), tab(CHUNK, LANES),
            tab(8, PAIR_W),
        ],
    )
    return pl.pallas_call(
        functools.partial(_attn_ret_kernel, tk=tk),
        grid_spec=grid_spec,
        out_shape=[jax.ShapeDtypeStruct((b, s, w), BF16), jax.ShapeDtypeStruct((b, s, v_w), BF16)],
        compiler_params=pltpu.CompilerParams(
            dimension_semantics=("arbitrary", "arbitrary", "arbitrary"),
            vmem_limit_bytes=VMEM_LIMIT),
        name="attention_retention",
    )(lgf, lgb, q, k, v, km, vm, rq, rk, rv, sg, sb, rkm, rvm, gnw)


def _merge_ffn_kernel(x_ref, yr_ref, ya_ref, gr_ref, gm_ref, w_ro, w_mo, w_o, nfw_ref,
                      w_g, w_u, w_d, nfin_ref, o_ref, *, ffn_chunks):
    def mm(a, w):
        return jnp.dot(a, w[...], preferred_element_type=F32)

    lane = lax.broadcasted_iota(jnp.int32, (ya_ref.shape[1], LANES), 1)
    packed = []
    for i in range(MLA_HEADS // 2):
        even = ya_ref[0, :, (2 * i) * LANES:(2 * i + 1) * LANES].astype(F32)
        odd = ya_ref[0, :, (2 * i + 1) * LANES:(2 * i + 2) * LANES].astype(F32)
        packed.append(jnp.where(lane < MLA_DV, even, pltpu.roll(odd, MLA_DV, axis=1)))
    ya = jnp.concatenate(packed, axis=1).astype(BF16)
    merged = (gr_ref[0].astype(F32) * mm(yr_ref[0], w_ro)
              + gm_ref[0].astype(F32) * mm(ya, w_mo))
    h1 = x_ref[0] + mm(merged.astype(BF16), w_o)
    u = (h1 * nfw_ref[...]).astype(BF16)
    inv_rms = lax.rsqrt(jnp.mean(h1 * h1, axis=-1, keepdims=True) + RMS_EPS)
    h2 = h1
    n_tiles = w_g.shape[1] // MXU_TILE
    edges = [MXU_TILE * ((n_tiles * c + ffn_chunks - 1) // ffn_chunks) for c in range(ffn_chunks + 1)]
    for lo, hi in zip(edges[:-1], edges[1:]):
        cols = slice(lo, hi)
        g = jnp.dot(u, w_g[:, cols], preferred_element_type=F32) * inv_rms
        up = jnp.dot(u, w_u[:, cols], preferred_element_type=F32) * inv_rms
        act = (g * _sigmoid(g) * up).astype(BF16)
        h2 = h2 + jnp.dot(act, w_d[cols, :], preferred_element_type=F32)
    o_ref[0] = _rms(h2, nfin_ref[...])


def _merge_ffn(x, yr, ya, gr, gm, wts, tm, ffn_chunks):
    b, s, d = x.shape
    const = lambda bi, i: (0, 0)
    wspec = lambda w: pl.BlockSpec(w.shape, const, pipeline_mode=pl.Buffered(1))
    row = pl.BlockSpec((1, tm, d), lambda bi, i: (bi, i, 0))
    return pl.pallas_call(
        functools.partial(_merge_ffn_kernel, ffn_chunks=ffn_chunks),
        grid=(b, s // tm),
        in_specs=[row] * 5 + [wspec(w) for w in wts],
        out_specs=row,
        out_shape=jax.ShapeDtypeStruct((b, s, d), F32),
        compiler_params=pltpu.CompilerParams(
            dimension_semantics=("arbitrary", "arbitrary"), vmem_limit_bytes=VMEM_LIMIT),
        name="merge_ffn",
    )(x, yr, ya, gr, gm, *wts)


def _rope_tables(n_pos):
    pos = np.arange(n_pos, dtype=np.float64)[:, None]
    lane = np.arange(LANES)

    def angles(half):
        inv = ROPE_BASE ** (-np.arange(half, dtype=np.float64) / half)
        return pos * inv[None, :]

    half = RET_DK // 2
    ang = angles(half)[:, lane % half]
    first = (lane % RET_DK) < half
    r_cos = np.cos(ang)
    r_lo = np.where(first, -np.sin(ang), 0.0)
    r_hi = np.where(first, 0.0, np.sin(ang))
    half = MLA_ROPE // 2
    ang = angles(half)[:, lane % half]
    in_rope = (lane >= MLA_NOPE) & (lane < MLA_QK)
    first = in_rope & (lane < MLA_NOPE + half)
    second = in_rope & (lane >= MLA_NOPE + half)
    m_cos = np.where(in_rope, np.cos(ang), 1.0)
    m_lo = np.where(first, -np.sin(ang), 0.0)
    m_hi = np.where(second, np.sin(ang), 0.0)
    return tuple(t.astype(np.float32) for t in (r_cos, r_lo, r_hi, m_cos, m_lo, m_hi))


def _pad_heads(w, heads, width):
    k = w.shape[0]
    w = w.reshape(k, heads, width)
    return jnp.pad(w, ((0, 0), (0, 0), (0, LANES - width))).reshape(k, heads * LANES)


def kernel(x, meta_tokens, norm_mix_w, w_in, ret_decay_fwd, ret_decay_bwd, ret_gn_w, w_ret_out, mla_q_norm_w, w_uq, mla_kv_norm_w, w_uk, w_uv, w_mla_out, w_o, norm_ffn_w, w_ffn_gate, w_ffn_up, w_ffn_down, norm_final_w):
    b, s, d = x.shape
    assert w_in.shape[0] == 1, "one layer: the meta rows are dropped after it"
    ret_qk_w = RET_HEADS * RET_DK
    ret_v_w = RET_HEADS * RET_DV
    q_rank = w_uq.shape[1]
    kv_rank = w_uk.shape[1]
    sizes = (ret_qk_w, ret_qk_w, ret_v_w, ret_v_w, q_rank, kv_rank, MLA_ROPE, d, d)
    assert sum(sizes) == w_in.shape[2]
    bounds = [0]
    for sz in sizes:
        bounds.append(bounds[-1] + sz)
    w_rq, w_rk, w_rv, w_rg, w_cq, w_ckv, w_kr, w_gr, w_gm = (
        w_in[0][:, lo:hi].astype(BF16) for lo, hi in zip(bounds[:-1], bounds[1:]))
    w_lat = jnp.concatenate(
        [w_cq, w_ckv, jnp.pad(w_kr, ((0, 0), (MLA_NOPE, LANES - MLA_QK)))], axis=1)
    row2 = lambda a: a.reshape(1, -1).astype(F32)
    wts1 = (w_rq, w_rk, w_rv, w_rg, w_lat, w_gr, w_gm,
            row2(mla_q_norm_w[0]), _pad_heads(w_uq[0], MLA_HEADS, MLA_QK).astype(BF16),
            row2(mla_kv_norm_w[0]),
            jnp.concatenate([w_uk[0], w_uv[0]], axis=1).astype(BF16))
    nw = row2(norm_mix_w[0])
    q_scale = (MLA_QK ** -0.5) * math.log2(math.e)

    tabs = _rope_tables(s + N_META)
    tabs_meta = tuple(t[:N_META] for t in tabs)
    tabs_x = tuple(t[N_META:] for t in tabs)
    lgf = -jnp.exp(ret_decay_fwd[0].astype(F32))
    lgb = -jnp.exp(ret_decay_bwd[0].astype(F32))
    rq, rk, rv, sg, q, k, v, gr, gm, sb = _in_proj(
        x, tabs_x, nw, wts1, lgb, q_scale, tm=IN_PROJ_ROWS, with_bwd_states=True)
    meta = meta_tokens.astype(x.dtype)[None]
    _, rk_m, rv_m, _, _, k_m, v_m, _, _ = _in_proj(
        meta, tabs_meta, nw, wts1, lgb, q_scale, tm=N_META, with_bwd_states=False)
    y_att, y_ret = _attention_retention(
        lgf, lgb, q, k, v, k_m[0], v_m[0], rq, rk, rv, sg, sb, rk_m[0], rv_m[0],
        row2(ret_gn_w[0]), tq=ATTN_Q_ROWS, tk=ATTN_KEY_TILE)

    wts4 = (w_ret_out[0].astype(BF16), w_mla_out[0].astype(BF16), w_o[0].astype(BF16),
            row2(norm_ffn_w[0]), w_ffn_gate[0].astype(BF16), w_ffn_up[0].astype(BF16),
            w_ffn_down[0].astype(BF16), row2(norm_final_w))
    return _merge_ffn(x, y_ret, y_att, gr, gm, wts4, tm=MERGE_ROWS, ffn_chunks=FFN_CHUNKS)
```

```python
import functools
import math

import jax
import jax.numpy as jnp
import numpy as np
from jax import lax
from jax.experimental import pallas as pl
from jax.experimental.pallas import tpu as pltpu

N_META = 16
CHUNK = 128
RET_HEADS = 8
RET_DK = 64
RET_DV = 128
MLA_HEADS = 8
MLA_NOPE = 64
MLA_ROPE = 32
MLA_DV = 64
MLA_QK = MLA_NOPE + MLA_ROPE
assert MLA_NOPE == MLA_DV, "k_nope and v share one 64-lanes-per-head up-projection layout"
ROPE_BASE = 10000.0
RMS_EPS = 1e-6
GN_EPS = 1e-5

LANES = 128
MXU_TILE = 256
ONES_LANE = MLA_DV
VMEM_LIMIT = 56 * 1024 * 1024

IN_PROJ_ROWS = 512
ATTN_Q_ROWS = 1024
ATTN_KEY_TILE = 2048
MERGE_ROWS = 512
FFN_CHUNKS = 4

F32 = jnp.float32
BF16 = jnp.bfloat16

_TN = (((0,), (0,)), ((), ()))
_NT = (((1,), (1,)), ((), ()))


def _rms(x, w):
    return x * lax.rsqrt(jnp.mean(x * x, axis=-1, keepdims=True) + RMS_EPS) * w


def _sigmoid(x):
    return 1.0 / (1.0 + jnp.exp(-x))


def _rope_group(x, cos, sin_lo, sin_hi, half):
    return (x * cos + pltpu.roll(x, LANES - half, axis=1) * sin_lo
            + pltpu.roll(x, half, axis=1) * sin_hi)


def _in_proj_kernel(x_ref, nw_ref, w_rq, w_rk, w_rv, w_rg, w_lat,
                    qnw_ref, w_uq, kvnw_ref, w_ukv,
                    rc_ref, rlo_ref, rhi_ref, mc_ref, mlo_ref, mhi_ref, lgb_ref,
                    rq_o, rk_o, rv_o, sg_o, q_o, k_o, v_o, *rest, q_scale):
    if rest:
        sb_o, sb_run, wkb, gcb = rest

        @pl.when(pl.program_id(1) == 0)
        def _start_of_sequence():
            for i in range(N_PAIRS):
                wkb[i] = jnp.exp(_pair_scalar(lgb_ref, i, _odd_lane()) * _chunk_pos())
                gcb[i] = jnp.exp(_pair_scalar(lgb_ref, i, _odd_col(8)) * CHUNK)
                sb_run[i] = jnp.zeros((LANES, PAIR_W), F32)

    x = x_ref[0]
    u = (x * nw_ref[...]).astype(BF16)
    inv_rms = lax.rsqrt(jnp.mean(x * x, axis=-1, keepdims=True) + RMS_EPS)

    def mm(a, w):
        return jnp.dot(a, w[...], preferred_element_type=F32)

    def mm_u(w):
        return mm(u, w) * inv_rms

    rc, rlo, rhi = rc_ref[...], rlo_ref[...], rhi_ref[...]
    mc, mlo, mhi = mc_ref[...], mlo_ref[...], mhi_ref[...]

    lat = mm_u(w_lat)
    q_rank = qnw_ref.shape[1]
    kv_rank = kvnw_ref.shape[1]
    cq = _rms(lat[:, :q_rank], qnw_ref[...]).astype(BF16)
    ckv = _rms(lat[:, q_rank:q_rank + kv_rank], kvnw_ref[...]).astype(BF16)
    kr = _rope_group(lat[:, q_rank + kv_rank:], mc, mlo, mhi, MLA_ROPE // 2)

    rk = mm_u(w_rk)
    rk_groups = []
    for c in range(rk.shape[1] // LANES):
        sl = slice(c * LANES, (c + 1) * LANES)
        rk_groups.append((_rope_group(rk[:, sl], rc, rlo, rhi, RET_DK // 2)
                          * (RET_DK ** -0.5)).astype(BF16))
        rk_o[0, :, sl] = rk_groups[c]
    rv = mm_u(w_rv).astype(BF16)
    rv_o[0] = rv

    if rest:
        for i in range(N_PAIRS):
            state = sb_run[i]
            for c in range(x_ref.shape[1] // CHUNK - 1, -1, -1):
                rows = slice(c * CHUNK, (c + 1) * CHUNK)
                sb_o[0, c, i] = state.astype(BF16)
                inc = _state_increment(rk_groups[i][rows], wkb[i],
                                       rv[rows, i * PAIR_W:(i + 1) * PAIR_W])
                state = gcb[i][0:1, :] * state + inc
            sb_run[i] = state

    q = mm(cq, w_uq)
    for h in range(MLA_HEADS):
        sl = slice(h * LANES, (h + 1) * LANES)
        q_o[0, :, sl] = (_rope_group(q[:, sl], mc, mlo, mhi, MLA_ROPE // 2) * q_scale).astype(BF16)
    kv = mm(ckv, w_ukv)
    half_w = kv.shape[1] // 2
    lane = lax.broadcasted_iota(jnp.int32, (kv.shape[0], LANES), 1)
    low = lane < MLA_DV
    ones_col = jnp.where(lane == ONES_LANE, 1.0, 0.0)
    for g in range(MLA_HEADS // 2):
        kg = kv[:, g * LANES:(g + 1) * LANES]
        vg = kv[:, half_w + g * LANES:half_w + (g + 1) * LANES]
        pieces = ((kg, vg), (pltpu.roll(kg, MLA_DV, axis=1), pltpu.roll(vg, MLA_DV, axis=1)))
        for j, (kh, vh) in enumerate(pieces):
            sl = slice((2 * g + j) * LANES, (2 * g + j + 1) * LANES)
            k_o[0, :, sl] = jnp.where(low, kh, kr).astype(BF16)
            v_o[0, :, sl] = jnp.where(low, vh, ones_col).astype(BF16)

    rq = mm_u(w_rq)
    for c in range(rq.shape[1] // LANES):
        sl = slice(c * LANES, (c + 1) * LANES)
        rq_o[0, :, sl] = _rope_group(rq[:, sl], rc, rlo, rhi, RET_DK // 2).astype(BF16)

    rg = mm_u(w_rg)
    sg_o[0] = (rg * _sigmoid(rg)).astype(BF16)


def _in_proj(rows, tabs, nw, wts, lgb, q_scale, tm, with_bwd_states):
    nb, r, d = rows.shape
    n_tiles = r // tm
    const = lambda b, i: (0, 0)

    def wspec(w):
        return pl.BlockSpec(w.shape, const, pipeline_mode=pl.Buffered(1))

    row_spec = lambda width: pl.BlockSpec((1, tm, width), lambda b, i: (b, n_tiles - 1 - i, 0))
    tab_spec = pl.BlockSpec((tm, LANES), lambda b, i: (n_tiles - 1 - i, 0))
    (w_rq, w_rk, w_rv, w_rg, w_lat, qnw, w_uq, kvnw, w_ukv) = wts
    mla_w = MLA_HEADS * LANES
    out_widths = (w_rq.shape[1], w_rk.shape[1], w_rv.shape[1], w_rg.shape[1],
                  mla_w, mla_w, mla_w)
    out_specs = [row_spec(w) for w in out_widths]
    out_shape = [jax.ShapeDtypeStruct((nb, r, w), BF16) for w in out_widths]
    scratch = []
    if with_bwd_states:
        cpb = tm // CHUNK
        out_specs.append(pl.BlockSpec((1, cpb, N_PAIRS, LANES, PAIR_W),
                                      lambda b, i: (b, n_tiles - 1 - i, 0, 0, 0)))
        out_shape.append(jax.ShapeDtypeStruct((nb, r // CHUNK, N_PAIRS, LANES, PAIR_W), BF16))
        scratch = [
            pltpu.VMEM((N_PAIRS, LANES, PAIR_W), F32),
            pltpu.VMEM((N_PAIRS, CHUNK, LANES), F32),
            pltpu.VMEM((N_PAIRS, 8, PAIR_W), F32),
        ]
    return pl.pallas_call(
        functools.partial(_in_proj_kernel, q_scale=q_scale),
        grid=(nb, n_tiles),
        in_specs=([row_spec(d), wspec(nw)] + [wspec(w) for w in wts] + [tab_spec] * 6
                  + [pl.BlockSpec(memory_space=pltpu.SMEM)]),
        out_specs=out_specs,
        out_shape=out_shape,
        scratch_shapes=scratch,
        compiler_params=pltpu.CompilerParams(
            dimension_semantics=("arbitrary", "arbitrary"), vmem_limit_bytes=VMEM_LIMIT),
        name="in_proj",
    )(rows, nw, *wts, *tabs, lgb)


N_PAIRS = RET_HEADS // 2
PAIR_W = 2 * RET_DV


def _odd_lane():
    return lax.broadcasted_iota(jnp.int32, (CHUNK, LANES), 1) >= RET_DK


def _odd_col(rows=CHUNK):
    return lax.broadcasted_iota(jnp.int32, (rows, PAIR_W), 1) >= RET_DV


def _own_head():
    row = lax.broadcasted_iota(jnp.int32, (LANES, PAIR_W), 0)
    return (row >= RET_DK) == _odd_col(LANES)


def _chunk_pos():
    return lax.broadcasted_iota(jnp.int32, (CHUNK, LANES), 0).astype(F32)


def _pair_scalar(ref, i, odd):
    return jnp.where(odd, ref[2 * i + 1], ref[2 * i])


def _state_increment(kp, w, vp):
    kw = (kp.astype(F32) * w).astype(BF16)
    kv = lax.dot_general(kw, vp, _TN, preferred_element_type=F32)
    return jnp.where(_own_head(), kv, 0.0)


def _retention_tables(lgf_ref, lgb_ref, dmask, wqf, wqb, wkf, gcf):
    pos = _chunk_pos()
    row2 = lax.broadcasted_iota(jnp.int32, (CHUNK, PAIR_W), 0)
    col2 = lax.broadcasted_iota(jnp.int32, (CHUNK, PAIR_W), 1)
    odd_col = _odd_col()
    rel = (row2 - jnp.where(odd_col, col2 - RET_DV, col2)).astype(F32)
    for i in range(N_PAIRS):
        lf = _pair_scalar(lgf_ref, i, _odd_lane())
        lb = _pair_scalar(lgb_ref, i, _odd_lane())
        lf2 = _pair_scalar(lgf_ref, i, odd_col)
        lb2 = _pair_scalar(lgb_ref, i, odd_col)
        dmask[i] = jnp.exp(jnp.where(rel >= 0, lf2 * rel, -lb2 * rel))
        wqf[i] = jnp.exp(lf * (pos + 1.0))
        wqb[i] = jnp.exp(lb * (CHUNK - pos))
        wkf[i] = jnp.exp(lf * (CHUNK - 1.0 - pos))
        gcf[i] = jnp.exp(_pair_scalar(lgf_ref, i, _odd_col(8)) * CHUNK)


def _retention_meta_state(lgf_ref, km_ref, vm_ref, sf_run):
    mrow = lax.broadcasted_iota(jnp.int32, (N_META, LANES), 0).astype(F32)
    odd_m = lax.broadcasted_iota(jnp.int32, (N_META, LANES), 1) >= RET_DK
    for i in range(N_PAIRS):
        wm = jnp.exp(_pair_scalar(lgf_ref, i, odd_m) * (N_META - 1.0 - mrow))
        kmw = (km_ref[:, i * LANES:(i + 1) * LANES].astype(F32) * wm).astype(BF16)
        kv = lax.dot_general(kmw, vm_ref[:, i * PAIR_W:(i + 1) * PAIR_W], _TN,
                             preferred_element_type=F32)
        sf_run[i] = jnp.where(_own_head(), kv, 0.0)


def _retention_chunk(q_ref, k_ref, v_ref, sg_ref, sb_ref, gnw_ref, o_ref,
                     sf_run, dmask, wqf, wqb, wkf, gcf):
    zeros = jnp.zeros((CHUNK, RET_DV), BF16)
    odd_lane = _odd_lane()
    for i in range(N_PAIRS):
        qb16 = q_ref[0, :, i * LANES:(i + 1) * LANES]
        kp = k_ref[0, :, i * LANES:(i + 1) * LANES]
        vp = v_ref[0, :, i * PAIR_W:(i + 1) * PAIR_W]
        qp = qb16.astype(F32)
        kpf = kp.astype(F32)
        k2 = jnp.concatenate([jnp.where(odd_lane, 0.0, kpf).astype(BF16),
                              jnp.where(odd_lane, kpf, 0.0).astype(BF16)], axis=0)
        s2 = lax.dot_general(qb16, k2, _NT, preferred_element_type=F32)
        v_diag = jnp.concatenate([
            jnp.concatenate([vp[:, :RET_DV], zeros], axis=1),
            jnp.concatenate([zeros, vp[:, RET_DV:]], axis=1)], axis=0)
        q_cross = jnp.concatenate([(qp * wqf[i]).astype(BF16),
                                   (qp * wqb[i]).astype(BF16)], axis=1)
        state = sf_run[i]
        states = jnp.concatenate([state.astype(BF16), sb_ref[0, 0, i]], axis=0)
        o2 = (jnp.dot((s2 * dmask[i]).astype(BF16), v_diag, preferred_element_type=F32)
              + jnp.dot(q_cross, states, preferred_element_type=F32))
        sf_run[i] = gcf[i][0:1, :] * state + _state_increment(kp, wkf[i], vp)
        for j in range(2):
            o = o2[:, j * RET_DV:(j + 1) * RET_DV]
            mu = jnp.mean(o, axis=-1, keepdims=True)
            d = o - mu
            var = jnp.mean(d * d, axis=-1, keepdims=True)
            yn = d * lax.rsqrt(var + GN_EPS)
            hs = slice((2 * i + j) * RET_DV, (2 * i + j + 1) * RET_DV)
            y = yn * gnw_ref[:, hs] * sg_ref[0, :, hs].astype(F32)
            o_ref[0, :, hs] = y.astype(BF16)


def _attn_ret_kernel(lgf_ref, lgb_ref, q_ref, k_ref, v_ref, km_ref, vm_ref,
                     rq_ref, rk_ref, rv_ref, sg_ref, sb_ref, rkm_ref, rvm_ref, gnw_ref,
                     o_ref, yr_ref, sf_run, dmask, wqf, wqb, wkf, gcf, *, tk):
    @pl.when((pl.program_id(1) == 0) & (pl.program_id(2) == 0))
    def _start_of_sequence():
        _retention_tables(lgf_ref, lgb_ref, dmask, wqf, wqb, wkf, gcf)
        _retention_meta_state(lgf_ref, rkm_ref, rvm_ref, sf_run)

    _retention_chunk(rq_ref, rk_ref, rv_ref, sg_ref, sb_ref, gnw_ref, yr_ref,
                     sf_run, dmask, wqf, wqb, wkf, gcf)

    q = q_ref[0]
    s0 = lax.dot_general(q, km_ref[...], _NT, preferred_element_type=F32)
    m0 = jnp.max(s0, axis=1, keepdims=True)
    acc0 = jnp.dot(jnp.exp2(s0 - m0).astype(BF16), vm_ref[...], preferred_element_type=F32)

    m, acc = m0, acc0
    for j in range(k_ref.shape[1] // tk):
        kt = k_ref[0, j * tk:(j + 1) * tk, :]
        vt = v_ref[0, j * tk:(j + 1) * tk, :]
        s = lax.dot_general(q, kt, _NT, preferred_element_type=F32)
        m_new = jnp.maximum(m, jnp.max(s, axis=1, keepdims=True))
        p = jnp.exp2(s - m_new).astype(BF16)
        acc = jnp.exp2(m - m_new) * acc + jnp.dot(p, vt, preferred_element_type=F32)
        m = m_new
    o_ref[0] = (acc / acc[:, ONES_LANE:ONES_LANE + 1]).astype(BF16)


def _attention_retention(lgf, lgb, q, k, v, km, vm, rq, rk, rv, sg, sb, rkm, rvm, gnw, tq, tk):
    b, s, w = q.shape
    heads = w // LANES
    n_q = s // tq
    assert heads * n_q == s // CHUNK, "one retention chunk per attention grid step"
    v_w = rv.shape[2]
    att = lambda bi, h, i, *_: (bi, i, h)
    kv = lambda bi, h, i, *_: (bi, 0, h)
    meta = lambda bi, h, i, *_: (0, h)
    chunk = lambda bi, h, i, *_: (bi, h * n_q + i, 0)
    const = lambda bi, h, i, *_: (0, 0)
    tab = lambda rows, width: pltpu.VMEM((N_PAIRS, rows, width), F32)
    grid_spec = pltpu.PrefetchScalarGridSpec(
        num_scalar_prefetch=2,
        grid=(b, heads, n_q),
        in_specs=[
            pl.BlockSpec((1, tq, LANES), att),
            pl.BlockSpec((1, s, LANES), kv),
            pl.BlockSpec((1, s, LANES), kv),
            pl.BlockSpec((N_META, LANES), meta),
            pl.BlockSpec((N_META, LANES), meta),
            pl.BlockSpec((1, CHUNK, rq.shape[2]), chunk),
            pl.BlockSpec((1, CHUNK, rk.shape[2]), chunk),
            pl.BlockSpec((1, CHUNK, v_w), chunk),
            pl.BlockSpec((1, CHUNK, v_w), chunk),
            pl.BlockSpec((1, 1, N_PAIRS, LANES, PAIR_W),
                         lambda bi, h, i, *_: (bi, h * n_q + i, 0, 0, 0)),
            pl.BlockSpec(rkm.shape, const),
            pl.BlockSpec(rvm.shape, const),
            pl.BlockSpec(gnw.shape, const),
        ],
        out_specs=[pl.BlockSpec((1, tq, LANES), att), pl.BlockSpec((1, CHUNK, v_w), chunk)],
        scratch_shapes=[
            tab(LANES, PAIR_W),
            tab(CHUNK, PAIR_W),
            tab(CHUNK, LANES), tab(CHUNK, LANES), tab(CHUNK, LANES),
            tab(8, PAIR_W),
        ],
    )
    return pl.pallas_call(
        functools.partial(_attn_ret_kernel, tk=tk),
        grid_spec=grid_spec,
        out_shape=[jax.ShapeDtypeStruct((b, s, w), BF16), jax.ShapeDtypeStruct((b, s, v_w), BF16)],
        compiler_params=pltpu.CompilerParams(
            dimension_semantics=("arbitrary", "arbitrary", "arbitrary"),
            vmem_limit_bytes=VMEM_LIMIT),
        name="attention_retention",
    )(lgf, lgb, q, k, v, km, vm, rq, rk, rv, sg, sb, rkm, rvm, gnw)


def _merge_ffn_kernel(x_ref, yr_ref, ya_ref, nmw_ref, w_gr, w_gm, w_ro, w_mo, w_o, nfw_ref,
                      w_g, w_u, w_d, nfin_ref, o_ref, *, ffn_chunks):
    def mm(a, w):
        return jnp.dot(a, w[...], preferred_element_type=F32)

    x = x_ref[0]
    ux = (x * nmw_ref[...]).astype(BF16)
    inv_rms_x = lax.rsqrt(jnp.mean(x * x, axis=-1, keepdims=True) + RMS_EPS)
    gate_ret = _sigmoid(mm(ux, w_gr) * inv_rms_x)
    gate_mla = _sigmoid(mm(ux, w_gm) * inv_rms_x)

    lane = lax.broadcasted_iota(jnp.int32, (ya_ref.shape[1], LANES), 1)
    packed = []
    for i in range(MLA_HEADS // 2):
        even = ya_ref[0, :, (2 * i) * LANES:(2 * i + 1) * LANES].astype(F32)
        odd = ya_ref[0, :, (2 * i + 1) * LANES:(2 * i + 2) * LANES].astype(F32)
        packed.append(jnp.where(lane < MLA_DV, even, pltpu.roll(odd, MLA_DV, axis=1)))
    ya = jnp.concatenate(packed, axis=1).astype(BF16)
    merged = gate_ret * mm(yr_ref[0], w_ro) + gate_mla * mm(ya, w_mo)
    h1 = x + mm(merged.astype(BF16), w_o)
    u = (h1 * nfw_ref[...]).astype(BF16)
    inv_rms = lax.rsqrt(jnp.mean(h1 * h1, axis=-1, keepdims=True) + RMS_EPS)
    h2 = h1
    n_tiles = w_g.shape[1] // MXU_TILE
    edges = [MXU_TILE * ((n_tiles * c + ffn_chunks - 1) // ffn_chunks) for c in range(ffn_chunks + 1)]
    for lo, hi in zip(edges[:-1], edges[1:]):
        cols = slice(lo, hi)
        g = jnp.dot(u, w_g[:, cols], preferred_element_type=F32) * inv_rms
        up = jnp.dot(u, w_u[:, cols], preferred_element_type=F32) * inv_rms
        act = (g * _sigmoid(g) * up).astype(BF16)
        h2 = h2 + jnp.dot(act, w_d[cols, :], preferred_element_type=F32)
    o_ref[0] = _rms(h2, nfin_ref[...])


def _merge_ffn(x, yr, ya, wts, tm, ffn_chunks):
    b, s, d = x.shape
    const = lambda bi, i: (0, 0)
    wspec = lambda w: pl.BlockSpec(w.shape, const, pipeline_mode=pl.Buffered(1))
    row = pl.BlockSpec((1, tm, d), lambda bi, i: (bi, i, 0))
    return pl.pallas_call(
        functools.partial(_merge_ffn_kernel, ffn_chunks=ffn_chunks),
        grid=(b, s // tm),
        in_specs=[row] * 3 + [wspec(w) for w in wts],
        out_specs=row,
        out_shape=jax.ShapeDtypeStruct((b, s, d), F32),
        compiler_params=pltpu.CompilerParams(
            dimension_semantics=("arbitrary", "arbitrary"), vmem_limit_bytes=VMEM_LIMIT),
        name="merge_ffn",
    )(x, yr, ya, *wts)


def _rope_tables(n_pos):
    pos = np.arange(n_pos, dtype=np.float64)[:, None]
    lane = np.arange(LANES)

    def angles(half):
        inv = ROPE_BASE ** (-np.arange(half, dtype=np.float64) / half)
        return pos * inv[None, :]

    half = RET_DK // 2
    ang = angles(half)[:, lane % half]
    first = (lane % RET_DK) < half
    r_cos = np.cos(ang)
    r_lo = np.where(first, -np.sin(ang), 0.0)
    r_hi = np.where(first, 0.0, np.sin(ang))
    half = MLA_ROPE // 2
    ang = angles(half)[:, lane % half]
    in_rope = (lane >= MLA_NOPE) & (lane < MLA_QK)
    first = in_rope & (lane < MLA_NOPE + half)
    second = in_rope & (lane >= MLA_NOPE + half)
    m_cos = np.where(in_rope, np.cos(ang), 1.0)
    m_lo = np.where(first, -np.sin(ang), 0.0)
    m_hi = np.where(second, np.sin(ang), 0.0)
    return tuple(t.astype(np.float32) for t in (r_cos, r_lo, r_hi, m_cos, m_lo, m_hi))


def _pad_heads(w, heads, width):
    k = w.shape[0]
    w = w.reshape(k, heads, width)
    return jnp.pad(w, ((0, 0), (0, 0), (0, LANES - width))).reshape(k, heads * LANES)


def kernel(x, meta_tokens, norm_mix_w, w_in, ret_decay_fwd, ret_decay_bwd, ret_gn_w, w_ret_out, mla_q_norm_w, w_uq, mla_kv_norm_w, w_uk, w_uv, w_mla_out, w_o, norm_ffn_w, w_ffn_gate, w_ffn_up, w_ffn_down, norm_final_w):
    b, s, d = x.shape
    assert w_in.shape[0] == 1, "one layer: the meta rows are dropped after it"
    ret_qk_w = RET_HEADS * RET_DK
    ret_v_w = RET_HEADS * RET_DV
    q_rank = w_uq.shape[1]
    kv_rank = w_uk.shape[1]
    sizes = (ret_qk_w, ret_qk_w, ret_v_w, ret_v_w, q_rank, kv_rank, MLA_ROPE, d, d)
    assert sum(sizes) == w_in.shape[2]
    bounds = [0]
    for sz in sizes:
        bounds.append(bounds[-1] + sz)
    w_rq, w_rk, w_rv, w_rg, w_cq, w_ckv, w_kr, w_gr, w_gm = (
        w_in[0][:, lo:hi].astype(BF16) for lo, hi in zip(bounds[:-1], bounds[1:]))
    w_lat = jnp.concatenate(
        [w_cq, w_ckv, jnp.pad(w_kr, ((0, 0), (MLA_NOPE, LANES - MLA_QK)))], axis=1)
    row2 = lambda a: a.reshape(1, -1).astype(F32)
    wts1 = (w_rq, w_rk, w_rv, w_rg, w_lat,
            row2(mla_q_norm_w[0]), _pad_heads(w_uq[0], MLA_HEADS, MLA_QK).astype(BF16),
            row2(mla_kv_norm_w[0]),
            jnp.concatenate([w_uk[0], w_uv[0]], axis=1).astype(BF16))
    nw = row2(norm_mix_w[0])
    q_scale = (MLA_QK ** -0.5) * math.log2(math.e)

    tabs = _rope_tables(s + N_META)
    tabs_meta = tuple(t[:N_META] for t in tabs)
    tabs_x = tuple(t[N_META:] for t in tabs)
    lgf = -jnp.exp(ret_decay_fwd[0].astype(F32))
    lgb = -jnp.exp(ret_decay_bwd[0].astype(F32))
    rq, rk, rv, sg, q, k, v, sb = _in_proj(
        x, tabs_x, nw, wts1, lgb, q_scale, tm=IN_PROJ_ROWS, with_bwd_states=True)
    meta = meta_tokens.astype(x.dtype)[None]
    _, rk_m, rv_m, _, _, k_m, v_m = _in_proj(
        meta, tabs_meta, nw, wts1, lgb, q_scale, tm=N_META, with_bwd_states=False)
    y_att, y_ret = _attention_retention(
        lgf, lgb, q, k, v, k_m[0], v_m[0], rq, rk, rv, sg, sb, rk_m[0], rv_m[0],
        row2(ret_gn_w[0]), tq=ATTN_Q_ROWS, tk=ATTN_KEY_TILE)

    wts4 = (nw, w_gr, w_gm,
            w_ret_out[0].astype(BF16), w_mla_out[0].astype(BF16), w_o[0].astype(BF16),
            row2(norm_ffn_w[0]), w_ffn_gate[0].astype(BF16), w_ffn_up[0].astype(BF16),
            w_ffn_down[0].astype(BF16), row2(norm_final_w))
    return _merge_ffn(x, y_ret, y_att, wts4, tm=MERGE_ROWS, ffn_chunks=FFN_CHUNKS)
```

```python
import functools
import math

import jax
import jax.numpy as jnp
import numpy as np
from jax import lax
from jax.experimental import pallas as pl
from jax.experimental.pallas import tpu as pltpu

N_META = 16
CHUNK = 128
RET_HEADS = 8
RET_DK = 64
RET_DV = 128
MLA_HEADS = 8
MLA_NOPE = 64
MLA_ROPE = 32
MLA_DV = 64
MLA_QK = MLA_NOPE + MLA_ROPE
assert MLA_NOPE == MLA_DV, "k_nope and v share one 64-lanes-per-head up-projection layout"
ROPE_BASE = 10000.0
RMS_EPS = 1e-6
GN_EPS = 1e-5

LANES = 128
MXU_TILE = 256
ONES_LANE = MLA_DV
VMEM_LIMIT = 56 * 1024 * 1024

IN_PROJ_ROWS = 512
ATTN_Q_ROWS = 1024
ATTN_KEY_TILE = 2048
MERGE_ROWS = 512
FFN_CHUNKS = 4

F32 = jnp.float32
BF16 = jnp.bfloat16

_TN = (((0,), (0,)), ((), ()))
_NT = (((1,), (1,)), ((), ()))


def _rms(x, w):
    return x * lax.rsqrt(jnp.mean(x * x, axis=-1, keepdims=True) + RMS_EPS) * w


def _sigmoid(x):
    return 1.0 / (1.0 + jnp.exp(-x))


def _rope_group(x, cos, sin_lo, sin_hi, half):
    return (x * cos + pltpu.roll(x, LANES - half, axis=1) * sin_lo
            + pltpu.roll(x, half, axis=1) * sin_hi)


def _spread_kv(kv, kr):
    half_w = kv.shape[1] // 2
    lane = lax.broadcasted_iota(jnp.int32, (kv.shape[0], LANES), 1)
    low = lane < MLA_DV
    ones_col = jnp.where(lane == ONES_LANE, 1.0, 0.0)
    out = []
    for g in range(MLA_HEADS // 2):
        kg = kv[:, g * LANES:(g + 1) * LANES]
        vg = kv[:, half_w + g * LANES:half_w + (g + 1) * LANES]
        pieces = ((kg, vg), (pltpu.roll(kg, MLA_DV, axis=1), pltpu.roll(vg, MLA_DV, axis=1)))
        for j, (kh, vh) in enumerate(pieces):
            sl = slice((2 * g + j) * LANES, (2 * g + j + 1) * LANES)
            out.append((sl, jnp.where(low, kh, kr).astype(BF16),
                        jnp.where(low, vh, ones_col).astype(BF16)))
    return out


def _in_proj_kernel(x_ref, nw_ref, w_rq, w_rk, w_rv, w_rg, w_lat, w_gr, w_gm,
                    qnw_ref, w_uq, kvnw_ref, w_ukv,
                    rc_ref, rlo_ref, rhi_ref, mc_ref, mlo_ref, mhi_ref,
                    meta_ref, trc_ref, trlo_ref, trhi_ref, tmc_ref, tmlo_ref, tmhi_ref, lgb_ref,
                    rq_o, rk_o, rv_o, sg_o, q_o, k_o, v_o, gr_o, gm_o,
                    sb_o, rkm_o, rvm_o, km_o, vm_o, sb_run, wkb, gcb, *, q_scale):
    def mm(a, w):
        return jnp.dot(a, w[...], preferred_element_type=F32)

    q_rank = qnw_ref.shape[1]
    kv_rank = kvnw_ref.shape[1]

    @pl.when((pl.program_id(0) == 0) & (pl.program_id(1) == 0))
    def _meta_rows():
        xm = meta_ref[...]
        um = (xm * nw_ref[...]).astype(BF16)
        inv_m = lax.rsqrt(jnp.mean(xm * xm, axis=-1, keepdims=True) + RMS_EPS)
        lat_m = mm(um, w_lat) * inv_m
        ckv_m = _rms(lat_m[:, q_rank:q_rank + kv_rank], kvnw_ref[...]).astype(BF16)
        kr_m = _rope_group(lat_m[:, q_rank + kv_rank:], tmc_ref[...], tmlo_ref[...],
                           tmhi_ref[...], MLA_ROPE // 2)
        rk_m = mm(um, w_rk) * inv_m
        for c in range(rk_m.shape[1] // LANES):
            sl = slice(c * LANES, (c + 1) * LANES)
            rkm_o[:, sl] = (_rope_group(rk_m[:, sl], trc_ref[...], trlo_ref[...], trhi_ref[...],
                                        RET_DK // 2) * (RET_DK ** -0.5)).astype(BF16)
        rvm_o[...] = (mm(um, w_rv) * inv_m).astype(BF16)
        for sl, kh, vh in _spread_kv(mm(ckv_m, w_ukv), kr_m):
            km_o[:, sl] = kh
            vm_o[:, sl] = vh

    @pl.when(pl.program_id(1) == 0)
    def _start_of_sequence():
        for i in range(N_PAIRS):
            wkb[i] = jnp.exp(_pair_scalar(lgb_ref, i, _odd_lane()) * _chunk_pos())
            gcb[i] = jnp.exp(_pair_scalar(lgb_ref, i, _odd_col(8)) * CHUNK)
            sb_run[i] = jnp.zeros((LANES, PAIR_W), F32)

    x = x_ref[0]
    u = (x * nw_ref[...]).astype(BF16)
    inv_rms = lax.rsqrt(jnp.mean(x * x, axis=-1, keepdims=True) + RMS_EPS)

    def mm_u(w):
        return mm(u, w) * inv_rms

    rc, rlo, rhi = rc_ref[...], rlo_ref[...], rhi_ref[...]
    mc, mlo, mhi = mc_ref[...], mlo_ref[...], mhi_ref[...]

    lat = mm_u(w_lat)
    cq = _rms(lat[:, :q_rank], qnw_ref[...]).astype(BF16)
    ckv = _rms(lat[:, q_rank:q_rank + kv_rank], kvnw_ref[...]).astype(BF16)
    kr = _rope_group(lat[:, q_rank + kv_rank:], mc, mlo, mhi, MLA_ROPE // 2)

    rk = mm_u(w_rk)
    rk_groups = []
    for c in range(rk.shape[1] // LANES):
        sl = slice(c * LANES, (c + 1) * LANES)
        rk_groups.append((_rope_group(rk[:, sl], rc, rlo, rhi, RET_DK // 2)
                          * (RET_DK ** -0.5)).astype(BF16))
        rk_o[0, :, sl] = rk_groups[c]
    rv = mm_u(w_rv).astype(BF16)
    rv_o[0] = rv

    for i in range(N_PAIRS):
        state = sb_run[i]
        for c in range(x_ref.shape[1] // CHUNK - 1, -1, -1):
            rows = slice(c * CHUNK, (c + 1) * CHUNK)
            sb_o[0, c, i] = state.astype(BF16)
            inc = _state_increment(rk_groups[i][rows], wkb[i],
                                   rv[rows, i * PAIR_W:(i + 1) * PAIR_W])
            state = gcb[i][0:1, :] * state + inc
        sb_run[i] = state

    q = mm(cq, w_uq)
    for h in range(MLA_HEADS):
        sl = slice(h * LANES, (h + 1) * LANES)
        q_o[0, :, sl] = (_rope_group(q[:, sl], mc, mlo, mhi, MLA_ROPE // 2) * q_scale).astype(BF16)
    for sl, kh, vh in _spread_kv(mm(ckv, w_ukv), kr):
        k_o[0, :, sl] = kh
        v_o[0, :, sl] = vh

    rq = mm_u(w_rq)
    for c in range(rq.shape[1] // LANES):
        sl = slice(c * LANES, (c + 1) * LANES)
        rq_o[0, :, sl] = _rope_group(rq[:, sl], rc, rlo, rhi, RET_DK // 2).astype(BF16)

    rg = mm_u(w_rg)
    sg_o[0] = (rg * _sigmoid(rg)).astype(BF16)
    gr_o[0] = _sigmoid(mm_u(w_gr)).astype(BF16)
    gm_o[0] = _sigmoid(mm_u(w_gm)).astype(BF16)


def _in_proj(rows, tabs, meta, tabs_meta, nw, wts, lgb, q_scale, tm):
    nb, r, d = rows.shape
    n_tiles = r // tm
    const = lambda b, i: (0, 0)

    def wspec(w):
        return pl.BlockSpec(w.shape, const, pipeline_mode=pl.Buffered(1))

    row_spec = lambda width: pl.BlockSpec((1, tm, width), lambda b, i: (b, n_tiles - 1 - i, 0))
    tab_spec = pl.BlockSpec((tm, LANES), lambda b, i: (n_tiles - 1 - i, 0))
    (w_rq, w_rk, w_rv, w_rg, w_lat, w_gr, w_gm, qnw, w_uq, kvnw, w_ukv) = wts
    mla_w = MLA_HEADS * LANES
    out_widths = (w_rq.shape[1], w_rk.shape[1], w_rv.shape[1], w_rg.shape[1],
                  mla_w, mla_w, mla_w, w_gr.shape[1], w_gm.shape[1])
    out_specs = [row_spec(w) for w in out_widths]
    out_shape = [jax.ShapeDtypeStruct((nb, r, w), BF16) for w in out_widths]
    cpb = tm // CHUNK
    out_specs.append(pl.BlockSpec((1, cpb, N_PAIRS, LANES, PAIR_W),
                                  lambda b, i: (b, n_tiles - 1 - i, 0, 0, 0)))
    out_shape.append(jax.ShapeDtypeStruct((nb, r // CHUNK, N_PAIRS, LANES, PAIR_W), BF16))
    n_meta = meta.shape[0]
    for w in (w_rk.shape[1], w_rv.shape[1], mla_w, mla_w):
        out_specs.append(pl.BlockSpec((n_meta, w), const))
        out_shape.append(jax.ShapeDtypeStruct((n_meta, w), BF16))
    scratch = [
        pltpu.VMEM((N_PAIRS, LANES, PAIR_W), F32),
        pltpu.VMEM((N_PAIRS, CHUNK, LANES), F32),
        pltpu.VMEM((N_PAIRS, 8, PAIR_W), F32),
    ]
    meta_spec = lambda a: pl.BlockSpec(a.shape, const)
    return pl.pallas_call(
        functools.partial(_in_proj_kernel, q_scale=q_scale),
        grid=(nb, n_tiles),
        in_specs=([row_spec(d), wspec(nw)] + [wspec(w) for w in wts] + [tab_spec] * 6
                  + [meta_spec(meta)] + [meta_spec(t) for t in tabs_meta]
                  + [pl.BlockSpec(memory_space=pltpu.SMEM)]),
        out_specs=out_specs,
        out_shape=out_shape,
        scratch_shapes=scratch,
        compiler_params=pltpu.CompilerParams(
            dimension_semantics=("arbitrary", "arbitrary"), vmem_limit_bytes=VMEM_LIMIT),
        name="in_proj",
    )(rows, nw, *wts, *tabs, meta, *tabs_meta, lgb)


N_PAIRS = RET_HEADS // 2
PAIR_W = 2 * RET_DV


def _odd_lane():
    return lax.broadcasted_iota(jnp.int32, (CHUNK, LANES), 1) >= RET_DK


def _odd_col(rows=CHUNK):
    return lax.broadcasted_iota(jnp.int32, (rows, PAIR_W), 1) >= RET_DV


def _own_head():
    row = lax.broadcasted_iota(jnp.int32, (LANES, PAIR_W), 0)
    return (row >= RET_DK) == _odd_col(LANES)


def _chunk_pos():
    return lax.broadcasted_iota(jnp.int32, (CHUNK, LANES), 0).astype(F32)


def _pair_scalar(ref, i, odd):
    return jnp.where(odd, ref[2 * i + 1], ref[2 * i])


def _state_increment(kp, w, vp):
    kw = (kp.astype(F32) * w).astype(BF16)
    kv = lax.dot_general(kw, vp, _TN, preferred_element_type=F32)
    return jnp.where(_own_head(), kv, 0.0)


def _retention_tables(lgf_ref, lgb_ref, dmask, wqf, wqb, wkf, gcf):
    pos = _chunk_pos()
    row2 = lax.broadcasted_iota(jnp.int32, (CHUNK, PAIR_W), 0)
    col2 = lax.broadcasted_iota(jnp.int32, (CHUNK, PAIR_W), 1)
    odd_col = _odd_col()
    rel = (row2 - jnp.where(odd_col, col2 - RET_DV, col2)).astype(F32)
    for i in range(N_PAIRS):
        lf = _pair_scalar(lgf_ref, i, _odd_lane())
        lb = _pair_scalar(lgb_ref, i, _odd_lane())
        lf2 = _pair_scalar(lgf_ref, i, odd_col)
        lb2 = _pair_scalar(lgb_ref, i, odd_col)
        dmask[i] = jnp.exp(jnp.where(rel >= 0, lf2 * rel, -lb2 * rel))
        wqf[i] = jnp.exp(lf * (pos + 1.0))
        wqb[i] = jnp.exp(lb * (CHUNK - pos))
        wkf[i] = jnp.exp(lf * (CHUNK - 1.0 - pos))
        gcf[i] = jnp.exp(_pair_scalar(lgf_ref, i, _odd_col(8)) * CHUNK)


def _retention_meta_state(lgf_ref, km_ref, vm_ref, sf_run):
    mrow = lax.broadcasted_iota(jnp.int32, (N_META, LANES), 0).astype(F32)
    odd_m = lax.broadcasted_iota(jnp.int32, (N_META, LANES), 1) >= RET_DK
    for i in range(N_PAIRS):
        wm = jnp.exp(_pair_scalar(lgf_ref, i, odd_m) * (N_META - 1.0 - mrow))
        kmw = (km_ref[:, i * LANES:(i + 1) * LANES].astype(F32) * wm).astype(BF16)
        kv = lax.dot_general(kmw, vm_ref[:, i * PAIR_W:(i + 1) * PAIR_W], _TN,
                             preferred_element_type=F32)
        sf_run[i] = jnp.where(_own_head(), kv, 0.0)


def _retention_chunk(q_ref, k_ref, v_ref, sg_ref, sb_ref, gnw_ref, o_ref,
                     sf_run, dmask, wqf, wqb, wkf, gcf):
    zeros = jnp.zeros((CHUNK, RET_DV), BF16)
    odd_lane = _odd_lane()
    for i in range(N_PAIRS):
        qb16 = q_ref[0, :, i * LANES:(i + 1) * LANES]
        kp = k_ref[0, :, i * LANES:(i + 1) * LANES]
        vp = v_ref[0, :, i * PAIR_W:(i + 1) * PAIR_W]
        qp = qb16.astype(F32)
        kpf = kp.astype(F32)
        k2 = jnp.concatenate([jnp.where(odd_lane, 0.0, kpf).astype(BF16),
                              jnp.where(odd_lane, kpf, 0.0).astype(BF16)], axis=0)
        s2 = lax.dot_general(qb16, k2, _NT, preferred_element_type=F32)
        v_diag = jnp.concatenate([
            jnp.concatenate([vp[:, :RET_DV], zeros], axis=1),
            jnp.concatenate([zeros, vp[:, RET_DV:]], axis=1)], axis=0)
        q_cross = jnp.concatenate([(qp * wqf[i]).astype(BF16),
                                   (qp * wqb[i]).astype(BF16)], axis=1)
        state = sf_run[i]
        states = jnp.concatenate([state.astype(BF16), sb_ref[0, 0, i]], axis=0)
        o2 = (jnp.dot((s2 * dmask[i]).astype(BF16), v_diag, preferred_element_type=F32)
              + jnp.dot(q_cross, states, preferred_element_type=F32))
        sf_run[i] = gcf[i][0:1, :] * state + _state_increment(kp, wkf[i], vp)
        for j in range(2):
            o = o2[:, j * RET_DV:(j + 1) * RET_DV]
            mu = jnp.mean(o, axis=-1, keepdims=True)
            d = o - mu
            var = jnp.mean(d * d, axis=-1, keepdims=True)
            yn = d * lax.rsqrt(var + GN_EPS)
            hs = slice((2 * i + j) * RET_DV, (2 * i + j + 1) * RET_DV)
            y = yn * gnw_ref[:, hs] * sg_ref[0, :, hs].astype(F32)
            o_ref[0, :, hs] = y.astype(BF16)


def _attn_ret_kernel(lgf_ref, lgb_ref, q_ref, k_ref, v_ref, km_ref, vm_ref,
                     rq_ref, rk_ref, rv_ref, sg_ref, sb_ref, rkm_ref, rvm_ref, gnw_ref,
                     o_ref, yr_ref, sf_run, dmask, wqf, wqb, wkf, gcf, *, tk):
    @pl.when((pl.program_id(1) == 0) & (pl.program_id(2) == 0))
    def _start_of_sequence():
        _retention_tables(lgf_ref, lgb_ref, dmask, wqf, wqb, wkf, gcf)
        _retention_meta_state(lgf_ref, rkm_ref, rvm_ref, sf_run)

    _retention_chunk(rq_ref, rk_ref, rv_ref, sg_ref, sb_ref, gnw_ref, yr_ref,
                     sf_run, dmask, wqf, wqb, wkf, gcf)

    q = q_ref[0]
    s0 = lax.dot_general(q, km_ref[...], _NT, preferred_element_type=F32)
    m0 = jnp.max(s0, axis=1, keepdims=True)
    acc0 = jnp.dot(jnp.exp2(s0 - m0).astype(BF16), vm_ref[...], preferred_element_type=F32)

    m, acc = m0, acc0
    for j in range(k_ref.shape[1] // tk):
        kt = k_ref[0, j * tk:(j + 1) * tk, :]
        vt = v_ref[0, j * tk:(j + 1) * tk, :]
        s = lax.dot_general(q, kt, _NT, preferred_element_type=F32)
        m_new = jnp.maximum(m, jnp.max(s, axis=1, keepdims=True))
        p = jnp.exp2(s - m_new).astype(BF16)
        acc = jnp.exp2(m - m_new) * acc + jnp.dot(p, vt, preferred_element_type=F32)
        m = m_new
    o_ref[0] = (acc / acc[:, ONES_LANE:ONES_LANE + 1]).astype(BF16)


def _attention_retention(lgf, lgb, q, k, v, km, vm, rq, rk, rv, sg, sb, rkm, rvm, gnw, tq, tk):
    b, s, w = q.shape
    heads = w // LANES
    n_q = s // tq
    assert heads * n_q == s // CHUNK, "one retention chunk per attention grid step"
    v_w = rv.shape[2]
    att = lambda bi, h, i, *_: (bi, i, h)
    kv = lambda bi, h, i, *_: (bi, 0, h)
    meta = lambda bi, h, i, *_: (0, h)
    chunk = lambda bi, h, i, *_: (bi, h * n_q + i, 0)
    const = lambda bi, h, i, *_: (0, 0)
    tab = lambda rows, width: pltpu.VMEM((N_PAIRS, rows, width), F32)
    grid_spec = pltpu.PrefetchScalarGridSpec(
        num_scalar_prefetch=2,
        grid=(b, heads, n_q),
        in_specs=[
            pl.BlockSpec((1, tq, LANES), att),
            pl.BlockSpec((1, s, LANES), kv),
            pl.BlockSpec((1, s, LANES), kv),
            pl.BlockSpec((N_META, LANES), meta),
            pl.BlockSpec((N_META, LANES), meta),
            pl.BlockSpec((1, CHUNK, rq.shape[2]), chunk),
            pl.BlockSpec((1, CHUNK, rk.shape[2]), chunk),
            pl.BlockSpec((1, CHUNK, v_w), chunk),
            pl.BlockSpec((1, CHUNK, v_w), chunk),
            pl.BlockSpec((1, 1, N_PAIRS, LANES, PAIR_W),
                         lambda bi, h, i, *_: (bi, h * n_q + i, 0, 0, 0)),
            pl.BlockSpec(rkm.shape, const),
            pl.BlockSpec(rvm.shape, const),
            pl.BlockSpec(gnw.shape, const),
        ],
        out_specs=[pl.BlockSpec((1, tq, LANES), att), pl.BlockSpec((1, CHUNK, v_w), chunk)],
        scratch_shapes=[
            tab(LANES, PAIR_W),
            tab(CHUNK, PAIR_W),
            tab(CHUNK, LANES), tab(CHUNK, LANES), tab(CHUNK, LANES),
            tab(8, PAIR_W),
        ],
    )
    return pl.pallas_call(
        functools.partial(_attn_ret_kernel, tk=tk),
        grid_spec=grid_spec,
        out_shape=[jax.ShapeDtypeStruct((b, s, w), BF16), jax.ShapeDtypeStruct((b, s, v_w), BF16)],
        compiler_params=pltpu.CompilerParams(
            dimension_semantics=("arbitrary", "arbitrary", "arbitrary"),
            vmem_limit_bytes=VMEM_LIMIT),
        name="attention_retention",
    )(lgf, lgb, q, k, v, km, vm, rq, rk, rv, sg, sb, rkm, rvm, gnw)


def _merge_ffn_kernel(x_ref, yr_ref, ya_ref, gr_ref, gm_ref, w_ro, w_mo, w_o, nfw_ref,
                      w_g, w_u, w_d, nfin_ref, o_ref, *, ffn_chunks):
    def mm(a, w):
        return jnp.dot(a, w[...], preferred_element_type=F32)

    lane = lax.broadcasted_iota(jnp.int32, (ya_ref.shape[1], LANES), 1)
    packed = []
    for i in range(MLA_HEADS // 2):
        even = ya_ref[0, :, (2 * i) * LANES:(2 * i + 1) * LANES].astype(F32)
        odd = ya_ref[0, :, (2 * i + 1) * LANES:(2 * i + 2) * LANES].astype(F32)
        packed.append(jnp.where(lane < MLA_DV, even, pltpu.roll(odd, MLA_DV, axis=1)))
    ya = jnp.concatenate(packed, axis=1).astype(BF16)
    merged = (gr_ref[0].astype(F32) * mm(yr_ref[0], w_ro)
              + gm_ref[0].astype(F32) * mm(ya, w_mo))
    h1 = x_ref[0] + mm(merged.astype(BF16), w_o)
    u = (h1 * nfw_ref[...]).astype(BF16)
    inv_rms = lax.rsqrt(jnp.mean(h1 * h1, axis=-1, keepdims=True) + RMS_EPS)
    h2 = h1
    n_tiles = w_g.shape[1] // MXU_TILE
    edges = [MXU_TILE * ((n_tiles * c + ffn_chunks - 1) // ffn_chunks) for c in range(ffn_chunks + 1)]
    for lo, hi in zip(edges[:-1], edges[1:]):
        cols = slice(lo, hi)
        g = jnp.dot(u, w_g[:, cols], preferred_element_type=F32) * inv_rms
        up = jnp.dot(u, w_u[:, cols], preferred_element_type=F32) * inv_rms
        act = (g * _sigmoid(g) * up).astype(BF16)
        h2 = h2 + jnp.dot(act, w_d[cols, :], preferred_element_type=F32)
    o_ref[0] = _rms(h2, nfin_ref[...])


def _merge_ffn(x, yr, ya, gr, gm, wts, tm, ffn_chunks):
    b, s, d = x.shape
    const = lambda bi, i: (0, 0)
    wspec = lambda w: pl.BlockSpec(w.shape, const, pipeline_mode=pl.Buffered(1))
    row = pl.BlockSpec((1, tm, d), lambda bi, i: (bi, i, 0))
    return pl.pallas_call(
        functools.partial(_merge_ffn_kernel, ffn_chunks=ffn_chunks),
        grid=(b, s // tm),
        in_specs=[row] * 5 + [wspec(w) for w in wts],
        out_specs=row,
        out_shape=jax.ShapeDtypeStruct((b, s, d), F32),
        compiler_params=pltpu.CompilerParams(
            dimension_semantics=("arbitrary", "arbitrary"), vmem_limit_bytes=VMEM_LIMIT),
        name="merge_ffn",
    )(x, yr, ya, gr, gm, *wts)


def _rope_tables(n_pos):
    pos = np.arange(n_pos, dtype=np.float64)[:, None]
    lane = np.arange(LANES)

    def angles(half):
        inv = ROPE_BASE ** (-np.arange(half, dtype=np.float64) / half)
        return pos * inv[None, :]

    half = RET_DK // 2
    ang = angles(half)[:, lane % half]
    first = (lane % RET_DK) < half
    r_cos = np.cos(ang)
    r_lo = np.where(first, -np.sin(ang), 0.0)
    r_hi = np.where(first, 0.0, np.sin(ang))
    half = MLA_ROPE // 2
    ang = angles(half)[:, lane % half]
    in_rope = (lane >= MLA_NOPE) & (lane < MLA_QK)
    first = in_rope & (lane < MLA_NOPE + half)
    second = in_rope & (lane >= MLA_NOPE + half)
    m_cos = np.where(in_rope, np.cos(ang), 1.0)
    m_lo = np.where(first, -np.sin(ang), 0.0)
    m_hi = np.where(second, np.sin(ang), 0.0)
    return tuple(t.astype(np.float32) for t in (r_cos, r_lo, r_hi, m_cos, m_lo, m_hi))


def _pad_heads(w, heads, width):
    k = w.shape[0]
    w = w.reshape(k, heads, width)
    return jnp.pad(w, ((0, 0), (0, 0), (0, LANES - width))).reshape(k, heads * LANES)


def kernel(x, meta_tokens, norm_mix_w, w_in, ret_decay_fwd, ret_decay_bwd, ret_gn_w, w_ret_out, mla_q_norm_w, w_uq, mla_kv_norm_w, w_uk, w_uv, w_mla_out, w_o, norm_ffn_w, w_ffn_gate, w_ffn_up, w_ffn_down, norm_final_w):
    b, s, d = x.shape
    assert w_in.shape[0] == 1, "one layer: the meta rows are dropped after it"
    ret_qk_w = RET_HEADS * RET_DK
    ret_v_w = RET_HEADS * RET_DV
    q_rank = w_uq.shape[1]
    kv_rank = w_uk.shape[1]
    sizes = (ret_qk_w, ret_qk_w, ret_v_w, ret_v_w, q_rank, kv_rank, MLA_ROPE, d, d)
    assert sum(sizes) == w_in.shape[2]
    bounds = [0]
    for sz in sizes:
        bounds.append(bounds[-1] + sz)
    w_rq, w_rk, w_rv, w_rg, w_cq, w_ckv, w_kr, w_gr, w_gm = (
        w_in[0][:, lo:hi].astype(BF16) for lo, hi in zip(bounds[:-1], bounds[1:]))
    w_lat = jnp.concatenate(
        [w_cq, w_ckv, jnp.pad(w_kr, ((0, 0), (MLA_NOPE, LANES - MLA_QK)))], axis=1)
    row2 = lambda a: a.reshape(1, -1).astype(F32)
    wts1 = (w_rq, w_rk, w_rv, w_rg, w_lat, w_gr, w_gm,
            row2(mla_q_norm_w[0]), _pad_heads(w_uq[0], MLA_HEADS, MLA_QK).astype(BF16),
            row2(mla_kv_norm_w[0]),
            jnp.concatenate([w_uk[0], w_uv[0]], axis=1).astype(BF16))
    nw = row2(norm_mix_w[0])
    q_scale = (MLA_QK ** -0.5) * math.log2(math.e)

    tabs = _rope_tables(s + N_META)
    tabs_meta = tuple(t[:N_META] for t in tabs)
    tabs_x = tuple(t[N_META:] for t in tabs)
    lgf = -jnp.exp(ret_decay_fwd[0].astype(F32))
    lgb = -jnp.exp(ret_decay_bwd[0].astype(F32))
    rq, rk, rv, sg, q, k, v, gr, gm, sb, rk_m, rv_m, k_m, v_m = _in_proj(
        x, tabs_x, meta_tokens.astype(x.dtype), tabs_meta, nw, wts1, lgb, q_scale,
        tm=IN_PROJ_ROWS)
    y_att, y_ret = _attention_retention(
        lgf, lgb, q, k, v, k_m, v_m, rq, rk, rv, sg, sb, rk_m, rv_m,
        row2(ret_gn_w[0]), tq=ATTN_Q_ROWS, tk=ATTN_KEY_TILE)

    wts4 = (w_ret_out[0].astype(BF16), w_mla_out[0].astype(BF16), w_o[0].astype(BF16),
            row2(norm_ffn_w[0]), w_ffn_gate[0].astype(BF16), w_ffn_up[0].astype(BF16),
            w_ffn_down[0].astype(BF16), row2(norm_final_w))
    return _merge_ffn(x, y_ret, y_att, gr, gm, wts4, tm=MERGE_ROWS, ffn_chunks=FFN_CHUNKS)
```

```python
import functools
import math

import jax
import jax.numpy as jnp
import numpy as np
from jax import lax
from jax.experimental import pallas as pl
from jax.experimental.pallas import tpu as pltpu

N_META = 16
CHUNK = 128
RET_HEADS = 8
RET_DK = 64
RET_DV = 128
MLA_HEADS = 8
MLA_NOPE = 64
MLA_ROPE = 32
MLA_DV = 64
MLA_QK = MLA_NOPE + MLA_ROPE
assert MLA_NOPE == MLA_DV, "k_nope and v share one 64-lanes-per-head up-projection layout"
ROPE_BASE = 10000.0
RMS_EPS = 1e-6
GN_EPS = 1e-5

LANES = 128
MXU_TILE = 256
ONES_LANE = MLA_DV
VMEM_LIMIT = 56 * 1024 * 1024

IN_PROJ_ROWS = 512
ATTN_Q_ROWS = 1024
ATTN_KEY_TILE = 2048
MERGE_ROWS = 512
FFN_CHUNKS = 4

F32 = jnp.float32
BF16 = jnp.bfloat16

_TN = (((0,), (0,)), ((), ()))
_NT = (((1,), (1,)), ((), ()))


def _rms(x, w):
    return x * lax.rsqrt(jnp.mean(x * x, axis=-1, keepdims=True) + RMS_EPS) * w


def _sigmoid(x):
    return 1.0 / (1.0 + jnp.exp(-x))


def _rope_group(x, cos, sin_lo, sin_hi, half):
    return (x * cos + pltpu.roll(x, LANES - half, axis=1) * sin_lo
            + pltpu.roll(x, half, axis=1) * sin_hi)


def _spread_kv(kv, kr):
    half_w = kv.shape[1] // 2
    lane = lax.broadcasted_iota(jnp.int32, (kv.shape[0], LANES), 1)
    low = lane < MLA_DV
    ones_col = jnp.where(lane == ONES_LANE, 1.0, 0.0)
    out = []
    for g in range(MLA_HEADS // 2):
        kg = kv[:, g * LANES:(g + 1) * LANES]
        vg = kv[:, half_w + g * LANES:half_w + (g + 1) * LANES]
        pieces = ((kg, vg), (pltpu.roll(kg, MLA_DV, axis=1), pltpu.roll(vg, MLA_DV, axis=1)))
        for j, (kh, vh) in enumerate(pieces):
            sl = slice((2 * g + j) * LANES, (2 * g + j + 1) * LANES)
            out.append((sl, jnp.where(low, kh, kr).astype(BF16),
                        jnp.where(low, vh, ones_col).astype(BF16)))
    return out


def _in_proj_kernel(x_ref, nw_ref, w_rq, w_rk, w_rv, w_rg, w_lat, w_gr, w_gm,
                    qnw_ref, w_uq, kvnw_ref, w_ukv,
                    rc_ref, rlo_ref, rhi_ref, mc_ref, mlo_ref, mhi_ref,
                    meta_ref, trc_ref, trlo_ref, trhi_ref, tmc_ref, tmlo_ref, tmhi_ref, lgb_ref,
                    rq_o, rk_o, rv_o, sg_o, q_o, k_o, v_o, gr_o, gm_o,
                    sb_o, rkm_o, rvm_o, km_o, vm_o, sb_run, wkb, gcb, *, q_scale):
    def mm(a, w):
        return jnp.dot(a, w[...], preferred_element_type=F32)

    q_rank = qnw_ref.shape[1]
    kv_rank = kvnw_ref.shape[1]

    @pl.when((pl.program_id(0) == 0) & (pl.program_id(1) == 0))
    def _meta_rows():
        xm = meta_ref[...]
        um = (xm * nw_ref[...]).astype(BF16)
        inv_m = lax.rsqrt(jnp.mean(xm * xm, axis=-1, keepdims=True) + RMS_EPS)
        lat_m = mm(um, w_lat) * inv_m
        ckv_m = _rms(lat_m[:, q_rank:q_rank + kv_rank], kvnw_ref[...]).astype(BF16)
        kr_m = _rope_group(lat_m[:, q_rank + kv_rank:], tmc_ref[...], tmlo_ref[...],
                           tmhi_ref[...], MLA_ROPE // 2)
        rk_m = mm(um, w_rk) * inv_m
        for c in range(rk_m.shape[1] // LANES):
            sl = slice(c * LANES, (c + 1) * LANES)
            rkm_o[:, sl] = (_rope_group(rk_m[:, sl], trc_ref[...], trlo_ref[...], trhi_ref[...],
                                        RET_DK // 2) * (RET_DK ** -0.5)).astype(BF16)
        rvm_o[...] = (mm(um, w_rv) * inv_m).astype(BF16)
        for sl, kh, vh in _spread_kv(mm(ckv_m, w_ukv), kr_m):
            km_o[:, sl] = kh
            vm_o[:, sl] = vh

    @pl.when(pl.program_id(1) == 0)
    def _start_of_sequence():
        for i in range(N_PAIRS):
            wkb[i] = jnp.exp(_pair_scalar(lgb_ref, i, _odd_lane()) * _chunk_pos())
            gcb[i] = jnp.exp(_pair_scalar(lgb_ref, i, _odd_col(8)) * CHUNK)
            sb_run[i] = jnp.zeros((LANES, PAIR_W), F32)

    x = x_ref[0]
    u = (x * nw_ref[...]).astype(BF16)
    inv_rms = lax.rsqrt(jnp.mean(x * x, axis=-1, keepdims=True) + RMS_EPS)

    def mm_u(w):
        return mm(u, w) * inv_rms

    rc, rlo, rhi = rc_ref[...], rlo_ref[...], rhi_ref[...]
    mc, mlo, mhi = mc_ref[...], mlo_ref[...], mhi_ref[...]

    lat = mm_u(w_lat)
    cq = _rms(lat[:, :q_rank], qnw_ref[...]).astype(BF16)
    ckv = _rms(lat[:, q_rank:q_rank + kv_rank], kvnw_ref[...]).astype(BF16)
    kr = _rope_group(lat[:, q_rank + kv_rank:], mc, mlo, mhi, MLA_ROPE // 2)

    rk = mm_u(w_rk)
    rk_groups = []
    for c in range(rk.shape[1] // LANES):
        sl = slice(c * LANES, (c + 1) * LANES)
        rk_groups.append((_rope_group(rk[:, sl], rc, rlo, rhi, RET_DK // 2)
                          * (RET_DK ** -0.5)).astype(BF16))
        rk_o[0, :, sl] = rk_groups[c]
    rv = mm_u(w_rv).astype(BF16)
    rv_o[0] = rv

    for i in range(N_PAIRS):
        state = sb_run[i]
        for c in range(x_ref.shape[1] // CHUNK - 1, -1, -1):
            rows = slice(c * CHUNK, (c + 1) * CHUNK)
            sb_o[0, c, i] = state.astype(BF16)
            inc = _state_increment(rk_groups[i][rows], wkb[i],
                                   rv[rows, i * PAIR_W:(i + 1) * PAIR_W])
            state = gcb[i][0:1, :] * state + inc
        sb_run[i] = state

    q = mm(cq, w_uq)
    for h in range(MLA_HEADS):
        sl = slice(h * LANES, (h + 1) * LANES)
        q_o[0, :, sl] = (_rope_group(q[:, sl], mc, mlo, mhi, MLA_ROPE // 2) * q_scale).astype(BF16)
    for sl, kh, vh in _spread_kv(mm(ckv, w_ukv), kr):
        k_o[0, :, sl] = kh
        v_o[0, :, sl] = vh

    rq = mm_u(w_rq)
    for c in range(rq.shape[1] // LANES):
        sl = slice(c * LANES, (c + 1) * LANES)
        rq_o[0, :, sl] = _rope_group(rq[:, sl], rc, rlo, rhi, RET_DK // 2).astype(BF16)

    rg = mm_u(w_rg)
    sg_o[0] = (rg * _sigmoid(rg)).astype(BF16)
    gr_o[0] = _sigmoid(mm_u(w_gr)).astype(BF16)
    gm_o[0] = _sigmoid(mm_u(w_gm)).astype(BF16)


def _in_proj(rows, tabs, meta, tabs_meta, nw, wts, lgb, q_scale, tm):
    nb, r, d = rows.shape
    n_tiles = r // tm
    const = lambda b, i: (0, 0)

    def wspec(w):
        return pl.BlockSpec(w.shape, const, pipeline_mode=pl.Buffered(1))

    row_spec = lambda width: pl.BlockSpec((1, tm, width), lambda b, i: (b, n_tiles - 1 - i, 0))
    tab_spec = pl.BlockSpec((tm, LANES), lambda b, i: (n_tiles - 1 - i, 0))
    (w_rq, w_rk, w_rv, w_rg, w_lat, w_gr, w_gm, qnw, w_uq, kvnw, w_ukv) = wts
    mla_w = MLA_HEADS * LANES
    out_widths = (w_rq.shape[1], w_rk.shape[1], w_rv.shape[1], w_rg.shape[1],
                  mla_w, mla_w, mla_w, w_gr.shape[1], w_gm.shape[1])
    out_specs = [row_spec(w) for w in out_widths]
    out_shape = [jax.ShapeDtypeStruct((nb, r, w), BF16) for w in out_widths]
    cpb = tm // CHUNK
    out_specs.append(pl.BlockSpec((1, cpb, N_PAIRS, LANES, PAIR_W),
                                  lambda b, i: (b, n_tiles - 1 - i, 0, 0, 0)))
    out_shape.append(jax.ShapeDtypeStruct((nb, r // CHUNK, N_PAIRS, LANES, PAIR_W), BF16))
    n_meta = meta.shape[0]
    for w in (w_rk.shape[1], w_rv.shape[1], mla_w, mla_w):
        out_specs.append(pl.BlockSpec((n_meta, w), const))
        out_shape.append(jax.ShapeDtypeStruct((n_meta, w), BF16))
    scratch = [
        pltpu.VMEM((N_PAIRS, LANES, PAIR_W), F32),
        pltpu.VMEM((N_PAIRS, CHUNK, LANES), F32),
        pltpu.VMEM((N_PAIRS, 8, PAIR_W), F32),
    ]
    meta_spec = lambda a: pl.BlockSpec(a.shape, const)
    return pl.pallas_call(
        functools.partial(_in_proj_kernel, q_scale=q_scale),
        grid=(nb, n_tiles),
        in_specs=([row_spec(d), wspec(nw)] + [wspec(w) for w in wts] + [tab_spec] * 6
                  + [meta_spec(meta)] + [meta_spec(t) for t in tabs_meta]
                  + [pl.BlockSpec(memory_space=pltpu.SMEM)]),
        out_specs=out_specs,
        out_shape=out_shape,
        scratch_shapes=scratch,
        compiler_params=pltpu.CompilerParams(
            dimension_semantics=("arbitrary", "arbitrary"), vmem_limit_bytes=VMEM_LIMIT),
        name="in_proj",
    )(rows, nw, *wts, *tabs, meta, *tabs_meta, lgb)


N_PAIRS = RET_HEADS // 2
PAIR_W = 2 * RET_DV


def _odd_lane():
    return lax.broadcasted_iota(jnp.int32, (CHUNK, LANES), 1) >= RET_DK


def _odd_col(rows=CHUNK):
    return lax.broadcasted_iota(jnp.int32, (rows, PAIR_W), 1) >= RET_DV


def _own_head():
    row = lax.broadcasted_iota(jnp.int32, (LANES, PAIR_W), 0)
    return (row >= RET_DK) == _odd_col(LANES)


def _chunk_pos():
    return lax.broadcasted_iota(jnp.int32, (CHUNK, LANES), 0).astype(F32)


def _pair_scalar(ref, i, odd):
    return jnp.where(odd, ref[2 * i + 1], ref[2 * i])


def _state_increment(kp, w, vp):
    kw = (kp.astype(F32) * w).astype(BF16)
    kv = lax.dot_general(kw, vp, _TN, preferred_element_type=F32)
    return jnp.where(_own_head(), kv, 0.0)


def _retention_tables(lgf_ref, lgb_ref, dmask, wqf, wqb, wkf, gcf):
    pos = _chunk_pos()
    row2 = lax.broadcasted_iota(jnp.int32, (CHUNK, PAIR_W), 0)
    col2 = lax.broadcasted_iota(jnp.int32, (CHUNK, PAIR_W), 1)
    odd_col = _odd_col()
    rel = (row2 - jnp.where(odd_col, col2 - RET_DV, col2)).astype(F32)
    for i in range(N_PAIRS):
        lf = _pair_scalar(lgf_ref, i, _odd_lane())
        lb = _pair_scalar(lgb_ref, i, _odd_lane())
        lf2 = _pair_scalar(lgf_ref, i, odd_col)
        lb2 = _pair_scalar(lgb_ref, i, odd_col)
        dmask[i] = jnp.exp(jnp.where(rel >= 0, lf2 * rel, -lb2 * rel))
        wqf[i] = jnp.exp(lf * (pos + 1.0))
        wqb[i] = jnp.exp(lb * (CHUNK - pos))
        wkf[i] = jnp.exp(lf * (CHUNK - 1.0 - pos))
        gcf[i] = jnp.exp(_pair_scalar(lgf_ref, i, _odd_col(8)) * CHUNK)


def _retention_meta_state(lgf_ref, km_ref, vm_ref, sf_run):
    mrow = lax.broadcasted_iota(jnp.int32, (N_META, LANES), 0).astype(F32)
    odd_m = lax.broadcasted_iota(jnp.int32, (N_META, LANES), 1) >= RET_DK
    for i in range(N_PAIRS):
        wm = jnp.exp(_pair_scalar(lgf_ref, i, odd_m) * (N_META - 1.0 - mrow))
        kmw = (km_ref[:, i * LANES:(i + 1) * LANES].astype(F32) * wm).astype(BF16)
        kv = lax.dot_general(kmw, vm_ref[:, i * PAIR_W:(i + 1) * PAIR_W], _TN,
                             preferred_element_type=F32)
        sf_run[i] = jnp.where(_own_head(), kv, 0.0)


def _retention_chunk(q_ref, k_ref, v_ref, sg_ref, sb_ref, gnw_ref, o_ref,
                     sf_run, dmask, wqf, wqb, wkf, gcf):
    zeros = jnp.zeros((CHUNK, RET_DV), BF16)
    odd_lane = _odd_lane()
    for i in range(N_PAIRS):
        qb16 = q_ref[0, :, i * LANES:(i + 1) * LANES]
        kp = k_ref[0, :, i * LANES:(i + 1) * LANES]
        vp = v_ref[0, :, i * PAIR_W:(i + 1) * PAIR_W]
        qp = qb16.astype(F32)
        kpf = kp.astype(F32)
        k2 = jnp.concatenate([jnp.where(odd_lane, 0.0, kpf).astype(BF16),
                              jnp.where(odd_lane, kpf, 0.0).astype(BF16)], axis=0)
        s2 = lax.dot_general(qb16, k2, _NT, preferred_element_type=F32)
        v_diag = jnp.concatenate([
            jnp.concatenate([vp[:, :RET_DV], zeros], axis=1),
            jnp.concatenate([zeros, vp[:, RET_DV:]], axis=1)], axis=0)
        q_cross = jnp.concatenate([(qp * wqf[i]).astype(BF16),
                                   (qp * wqb[i]).astype(BF16)], axis=1)
        state = sf_run[i]
        states = jnp.concatenate([state.astype(BF16), sb_ref[0, 0, i]], axis=0)
        o2 = (jnp.dot((s2 * dmask[i]).astype(BF16), v_diag, preferred_element_type=F32)
              + jnp.dot(q_cross, states, preferred_element_type=F32))
        sf_run[i] = gcf[i][0:1, :] * state + _state_increment(kp, wkf[i], vp)
        for j in range(2):
            o = o2[:, j * RET_DV:(j + 1) * RET_DV]
            mu = jnp.mean(o, axis=-1, keepdims=True)
            d = o - mu
            var = jnp.mean(d * d, axis=-1, keepdims=True)
            yn = d * lax.rsqrt(var + GN_EPS)
            hs = slice((2 * i + j) * RET_DV, (2 * i + j + 1) * RET_DV)
            y = yn * gnw_ref[:, hs] * sg_ref[0, :, hs].astype(F32)
            o_ref[0, :, hs] = y.astype(BF16)


def _attn_ret_kernel(lgf_ref, lgb_ref, q_ref, k_ref, v_ref, km_ref, vm_ref,
                     rq_ref, rk_ref, rv_ref, sg_ref, sb_ref, rkm_ref, rvm_ref, gnw_ref,
                     o_ref, yr_ref, sf_run, dmask, wqf, wqb, wkf, gcf, *, tk):
    @pl.when((pl.program_id(1) == 0) & (pl.program_id(2) == 0))
    def _start_of_sequence():
        _retention_tables(lgf_ref, lgb_ref, dmask, wqf, wqb, wkf, gcf)
        _retention_meta_state(lgf_ref, rkm_ref, rvm_ref, sf_run)

    _retention_chunk(rq_ref, rk_ref, rv_ref, sg_ref, sb_ref, gnw_ref, yr_ref,
                     sf_run, dmask, wqf, wqb, wkf, gcf)

    q = q_ref[0]
    s0 = lax.dot_general(q, km_ref[...], _NT, preferred_element_type=F32)
    m0 = jnp.max(s0, axis=1, keepdims=True)
    acc0 = jnp.dot(jnp.exp2(s0 - m0).astype(BF16), vm_ref[...], preferred_element_type=F32)

    m, acc = m0, acc0
    for j in range(k_ref.shape[1] // tk):
        kt = k_ref[0, j * tk:(j + 1) * tk, :]
        vt = v_ref[0, j * tk:(j + 1) * tk, :]
        s = lax.dot_general(q, kt, _NT, preferred_element_type=F32)
        m_new = jnp.maximum(m, jnp.max(s, axis=1, keepdims=True))
        p = jnp.exp2(s - m_new).astype(BF16)
        acc = jnp.exp2(m - m_new) * acc + jnp.dot(p, vt, preferred_element_type=F32)
        m = m_new
    o_ref[0] = (acc / acc[:, ONES_LANE:ONES_LANE + 1]).astype(BF16)


def _attention_retention(lgf, lgb, q, k, v, km, vm, rq, rk, rv, sg, sb, rkm, rvm, gnw, tq, tk):
    b, s, w = q.shape
    heads = w // LANES
    n_q = s // tq
    assert heads * n_q == s // CHUNK, "one retention chunk per attention grid step"
    v_w = rv.shape[2]
    att = lambda bi, h, i, *_: (bi, i, h)
    kv = lambda bi, h, i, *_: (bi, 0, h)
    meta = lambda bi, h, i, *_: (0, h)
    chunk = lambda bi, h, i, *_: (bi, h * n_q + i, 0)
    const = lambda bi, h, i, *_: (0, 0)
    tab = lambda rows, width: pltpu.VMEM((N_PAIRS, rows, width), F32)
    grid_spec = pltpu.PrefetchScalarGridSpec(
        num_scalar_prefetch=2,
        grid=(b, heads, n_q),
        in_specs=[
            pl.BlockSpec((1, tq, LANES), att),
            pl.BlockSpec((1, s, LANES), kv),
            pl.BlockSpec((1, s, LANES), kv),
            pl.BlockSpec((N_META, LANES), meta),
            pl.BlockSpec((N_META, LANES), meta),
            pl.BlockSpec((1, CHUNK, rq.shape[2]), chunk),
            pl.BlockSpec((1, CHUNK, rk.shape[2]), chunk),
            pl.BlockSpec((1, CHUNK, v_w), chunk),
            pl.BlockSpec((1, CHUNK, v_w), chunk),
            pl.BlockSpec((1, 1, N_PAIRS, LANES, PAIR_W),
                         lambda bi, h, i, *_: (bi, h * n_q + i, 0, 0, 0)),
            pl.BlockSpec(rkm.shape, const),
            pl.BlockSpec(rvm.shape, const),
            pl.BlockSpec(gnw.shape, const),
        ],
        out_specs=[pl.BlockSpec((1, tq, LANES), att), pl.BlockSpec((1, CHUNK, v_w), chunk)],
        scratch_shapes=[
            tab(LANES, PAIR_W),
            tab(CHUNK, PAIR_W),
            tab(CHUNK, LANES), tab(CHUNK, LANES), tab(CHUNK, LANES),
            tab(8, PAIR_W),
        ],
    )
    return pl.pallas_call(
        functools.partial(_attn_ret_kernel, tk=tk),
        grid_spec=grid_spec,
        out_shape=[jax.ShapeDtypeStruct((b, s, w), BF16), jax.ShapeDtypeStruct((b, s, v_w), BF16)],
        compiler_params=pltpu.CompilerParams(
            dimension_semantics=("arbitrary", "arbitrary", "arbitrary"),
            vmem_limit_bytes=VMEM_LIMIT),
        name="attention_retention",
    )(lgf, lgb, q, k, v, km, vm, rq, rk, rv, sg, sb, rkm, rvm, gnw)


def _merge_ffn_kernel(x_ref, yr_ref, ya_ref, gr_ref, gm_ref, w_ro, w_mo, w_o, nfw_ref,
                      w_g, w_u, w_d, nfin_ref, o_ref, *, ffn_chunks):
    def mm(a, w):
        return jnp.dot(a, w[...], preferred_element_type=F32)

    lane = lax.broadcasted_iota(jnp.int32, (ya_ref.shape[1], LANES), 1)
    packed = []
    for i in range(MLA_HEADS // 2):
        even = ya_ref[0, :, (2 * i) * LANES:(2 * i + 1) * LANES].astype(F32)
        odd = ya_ref[0, :, (2 * i + 1) * LANES:(2 * i + 2) * LANES].astype(F32)
        packed.append(jnp.where(lane < MLA_DV, even, pltpu.roll(odd, MLA_DV, axis=1)))
    ya = jnp.concatenate(packed, axis=1).astype(BF16)
    merged = (gr_ref[0].astype(F32) * mm(yr_ref[0], w_ro)
              + gm_ref[0].astype(F32) * mm(ya, w_mo))
    h1 = x_ref[0] + mm(merged.astype(BF16), w_o)
    u = (h1 * nfw_ref[...]).astype(BF16)
    inv_rms = lax.rsqrt(jnp.mean(h1 * h1, axis=-1, keepdims=True) + RMS_EPS)
    h2 = h1
    n_tiles = w_g.shape[1] // MXU_TILE
    edges = [MXU_TILE * ((n_tiles * c + ffn_chunks - 1) // ffn_chunks) for c in range(ffn_chunks + 1)]
    for lo, hi in zip(edges[:-1], edges[1:]):
        cols = slice(lo, hi)
        g = jnp.dot(u, w_g[:, cols], preferred_element_type=F32) * inv_rms
        up = jnp.dot(u, w_u[:, cols], preferred_element_type=F32) * inv_rms
        act = (g * _sigmoid(g) * up).astype(BF16)
        h2 = h2 + jnp.dot(act, w_d[cols, :], preferred_element_type=F32)
    o_ref[0] = _rms(h2, nfin_ref[...])


def _merge_ffn(x, yr, ya, gr, gm, wts, tm, ffn_chunks):
    b, s, d = x.shape
    const = lambda bi, i: (0, 0)
    wspec = lambda w: pl.BlockSpec(w.shape, const, pipeline_mode=pl.Buffered(1))
    row = pl.BlockSpec((1, tm, d), lambda bi, i: (bi, i, 0))
    return pl.pallas_call(
        functools.partial(_merge_ffn_kernel, ffn_chunks=ffn_chunks),
        grid=(b, s // tm),
        in_specs=[row] * 5 + [wspec(w) for w in wts],
        out_specs=row,
        out_shape=jax.ShapeDtypeStruct((b, s, d), F32),
        compiler_params=pltpu.CompilerParams(
            dimension_semantics=("parallel", "parallel"), vmem_limit_bytes=VMEM_LIMIT),
        name="merge_ffn",
    )(x, yr, ya, gr, gm, *wts)


def _rope_tables(n_pos):
    pos = np.arange(n_pos, dtype=np.float64)[:, None]
    lane = np.arange(LANES)

    def angles(half):
        inv = ROPE_BASE ** (-np.arange(half, dtype=np.float64) / half)
        return pos * inv[None, :]

    half = RET_DK // 2
    ang = angles(half)[:, lane % half]
    first = (lane % RET_DK) < half
    r_cos = np.cos(ang)
    r_lo = np.where(first, -np.sin(ang), 0.0)
    r_hi = np.where(first, 0.0, np.sin(ang))
    half = MLA_ROPE // 2
    ang = angles(half)[:, lane % half]
    in_rope = (lane >= MLA_NOPE) & (lane < MLA_QK)
    first = in_rope & (lane < MLA_NOPE + half)
    second = in_rope & (lane >= MLA_NOPE + half)
    m_cos = np.where(in_rope, np.cos(ang), 1.0)
    m_lo = np.where(first, -np.sin(ang), 0.0)
    m_hi = np.where(second, np.sin(ang), 0.0)
    return tuple(t.astype(np.float32) for t in (r_cos, r_lo, r_hi, m_cos, m_lo, m_hi))


def _pad_heads(w, heads, width):
    k = w.shape[0]
    w = w.reshape(k, heads, width)
    return jnp.pad(w, ((0, 0), (0, 0), (0, LANES - width))).reshape(k, heads * LANES)


def kernel(x, meta_tokens, norm_mix_w, w_in, ret_decay_fwd, ret_decay_bwd, ret_gn_w, w_ret_out, mla_q_norm_w, w_uq, mla_kv_norm_w, w_uk, w_uv, w_mla_out, w_o, norm_ffn_w, w_ffn_gate, w_ffn_up, w_ffn_down, norm_final_w):
    b, s, d = x.shape
    assert w_in.shape[0] == 1, "one layer: the meta rows are dropped after it"
    ret_qk_w = RET_HEADS * RET_DK
    ret_v_w = RET_HEADS * RET_DV
    q_rank = w_uq.shape[1]
    kv_rank = w_uk.shape[1]
    sizes = (ret_qk_w, ret_qk_w, ret_v_w, ret_v_w, q_rank, kv_rank, MLA_ROPE, d, d)
    assert sum(sizes) == w_in.shape[2]
    bounds = [0]
    for sz in sizes:
        bounds.append(bounds[-1] + sz)
    w_rq, w_rk, w_rv, w_rg, w_cq, w_ckv, w_kr, w_gr, w_gm = (
        w_in[0][:, lo:hi].astype(BF16) for lo, hi in zip(bounds[:-1], bounds[1:]))
    w_lat = jnp.concatenate(
        [w_cq, w_ckv, jnp.pad(w_kr, ((0, 0), (MLA_NOPE, LANES - MLA_QK)))], axis=1)
    row2 = lambda a: a.reshape(1, -1).astype(F32)
    wts1 = (w_rq, w_rk, w_rv, w_rg, w_lat, w_gr, w_gm,
            row2(mla_q_norm_w[0]), _pad_heads(w_uq[0], MLA_HEADS, MLA_QK).astype(BF16),
            row2(mla_kv_norm_w[0]),
            jnp.concatenate([w_uk[0], w_uv[0]], axis=1).astype(BF16))
    nw = row2(norm_mix_w[0])
    q_scale = (MLA_QK ** -0.5) * math.log2(math.e)

    tabs = _rope_tables(s + N_META)
    tabs_meta = tuple(t[:N_META] for t in tabs)
    tabs_x = tuple(t[N_META:] for t in tabs)
    lgf = -jnp.exp(ret_decay_fwd[0].astype(F32))
    lgb = -jnp.exp(ret_decay_bwd[0].astype(F32))
    rq, rk, rv, sg, q, k, v, gr, gm, sb, rk_m, rv_m, k_m, v_m = _in_proj(
        x, tabs_x, meta_tokens.astype(x.dtype), tabs_meta, nw, wts1, lgb, q_scale,
        tm=IN_PROJ_ROWS)
    y_att, y_ret = _attention_retention(
        lgf, lgb, q, k, v, k_m, v_m, rq, rk, rv, sg, sb, rk_m, rv_m,
        row2(ret_gn_w[0]), tq=ATTN_Q_ROWS, tk=ATTN_KEY_TILE)

    wts4 = (w_ret_out[0].astype(BF16), w_mla_out[0].astype(BF16), w_o[0].astype(BF16),
            row2(norm_ffn_w[0]), w_ffn_gate[0].astype(BF16), w_ffn_up[0].astype(BF16),
            w_ffn_down[0].astype(BF16), row2(norm_final_w))
    return _merge_ffn(x, y_ret, y_att, gr, gm, wts4, tm=MERGE_ROWS, ffn_chunks=FFN_CHUNKS)
```
